```python
import jax, jax.numpy as jnp
from jax import lax
import numpy as np

D_MODEL = 2048
BATCH = 2
SEQ = 4096
DEPTH = 2
DEC_BATCH = 2
DEC_SEQ = 16384
PAST_LEN = 128

GRID_W = 64
HEAD_DIM = 128
N_HEADS = D_MODEL // HEAD_DIM
A_HEADS = N_HEADS // 2
A_KV_HEADS = 2
B_HEADS = N_HEADS - A_HEADS
B_KV_HEADS = 2
C_HEADS = N_HEADS
QBLK = 128
WINDOW = 128
NA_ROWS = 8
NA_COLS = 16
N_EXPERTS = 16
EC_CAPACITY_FACTOR = 2
D_FF = 5632
ROPE_THETA = 10000.0
NORM_EPS = 1e-6
NEG_INF = -1e30
N_EVEN = (DEPTH + 1) // 2
N_ODD = DEPTH // 2
AB_WIDTH = (A_HEADS + 2 * A_KV_HEADS + B_HEADS + 2 * B_KV_HEADS) * HEAD_DIM
C_WIDTH = 3 * C_HEADS * HEAD_DIM

kernel_name = "hybrid_axial_window_natten_ec_encoder"


def rms_norm(x, g):
    xf = x.astype(jnp.float32)
    y = xf * lax.rsqrt(jnp.mean(xf * xf, axis=-1, keepdims=True) + NORM_EPS)
    return (y * g.astype(jnp.float32)).astype(x.dtype)


def split_cols(x, widths):
    bounds = []
    acc = 0
    for w in widths[:-1]:
        acc += w
        bounds.append(acc)
    return jnp.split(x, bounds, axis=-1)


def alibi_slopes(n):
    return 2.0 ** (-8.0 * jnp.arange(1, n + 1, dtype=jnp.float32) / n)


def axial_rope(x):
    T = x.shape[1]
    quarter = HEAD_DIM // 4
    t = jnp.arange(T)
    inv = ROPE_THETA ** (-jnp.arange(quarter, dtype=jnp.float32) / quarter)
    ang_r = (t // GRID_W).astype(jnp.float32)[:, None] * inv
    ang_c = (t % GRID_W).astype(jnp.float32)[:, None] * inv

    def rot(xh, ang):
        x1, x2 = jnp.split(xh, 2, axis=-1)
        c = jnp.cos(ang)[None, :, None, :]
        s = jnp.sin(ang)[None, :, None, :]
        return jnp.concatenate([x1 * c - x2 * s, x2 * c + x1 * s], axis=-1)

    xr, xc = jnp.split(x.astype(jnp.float32), 2, axis=-1)
    return jnp.concatenate([rot(xr, ang_r), rot(xc, ang_c)], axis=-1).astype(x.dtype)


def global_attention(q, k, v):
    B, T, H, hd = q.shape
    KVH = k.shape[2]
    G = H // KVH
    nb = T // QBLK
    scale = hd ** -0.5
    qb = jnp.moveaxis(q.reshape(B, nb, QBLK, KVH, G, hd), 1, 0)

    def block(qblk):
        s = jnp.einsum('bqkgd,bskd->bkgqs', qblk, k).astype(jnp.float32) * scale
        p = jax.nn.softmax(s, axis=-1).astype(v.dtype)
        return jnp.einsum('bkgqs,bskd->bqkgd', p, v)

    o = lax.map(block, qb)
    return jnp.moveaxis(o, 0, 1).reshape(B, T, H * hd)


def window_sink_attention(q, k, v, sink, slopes):
    B, T, H, hd = q.shape
    KVH = k.shape[2]
    G = H // KVH
    nb = T // QBLK
    span = QBLK + 2 * WINDOW
    scale = hd ** -0.5
    pad = ((0, 0), (WINDOW, WINDOW), (0, 0), (0, 0))
    kp = jnp.pad(k, pad)
    vp = jnp.pad(v, pad)
    qb = jnp.moveaxis(q.reshape(B, nb, QBLK, KVH, G, hd), 1, 0)
    a = jnp.arange(QBLK)[:, None]
    j = jnp.arange(span)[None, :]
    rel = a + WINDOW - j
    in_band = jnp.abs(rel) <= WINDOW
    alibi = -slopes.reshape(KVH, G)[:, :, None, None] * jnp.abs(rel).astype(jnp.float32)
    sink_l = sink.astype(jnp.float32).reshape(KVH, G)[None, :, :, None, None]

    def block(args):
        i, qblk = args
        start = i * QBLK
        kw = lax.dynamic_slice_in_dim(kp, start, span, axis=1)
        vw = lax.dynamic_slice_in_dim(vp, start, span, axis=1)
        s_pos = start - WINDOW + j
        mask = in_band & (s_pos >= 0) & (s_pos < T)
        s = jnp.einsum('bqkgd,bskd->bkgqs', qblk, kw).astype(jnp.float32) * scale + alibi
        s = jnp.where(mask, s, NEG_INF)
        m = jnp.maximum(jnp.max(s, axis=-1, keepdims=True), sink_l)
        p = jnp.exp(s - m)
        denom = jnp.sum(p, axis=-1, keepdims=True) + jnp.exp(sink_l - m)
        p = (p / denom).astype(v.dtype)
        return jnp.einsum('bkgqs,bskd->bqkgd', p, vw)

    o = lax.map(block, (jnp.arange(nb), qb))
    return jnp.moveaxis(o, 0, 1).reshape(B, T, H * hd)


def neighborhood_attention(q, k, v, rpb):
    B, T, H, hd = q.shape
    rows = T // GRID_W
    kr = min(NA_ROWS, rows)
    ncb = GRID_W // NA_COLS
    kc = 2 * NA_COLS
    scale = hd ** -0.5
    qcol = jnp.arange(GRID_W).reshape(ncb, NA_COLS)
    cb = jnp.clip(jnp.arange(ncb) * NA_COLS - NA_COLS // 2, 0, GRID_W - kc)
    kcol = cb[:, None] + jnp.arange(kc)[None, :]
    cs = jnp.clip(qcol - NA_COLS // 2, 0, GRID_W - NA_COLS)
    kc3 = kcol[:, None, :]
    col_mask = (kc3 >= cs[:, :, None]) & (kc3 < cs[:, :, None] + NA_COLS)
    mask = col_mask[:, :, None, :]
    dc_idx = jnp.clip(kc3 - qcol[:, :, None], -(NA_COLS - 1), NA_COLS - 1) + (NA_COLS - 1)
    qg = jnp.moveaxis(q.reshape(B, rows, ncb, NA_COLS, H, hd), 1, 0)
    kg = k.reshape(B, rows, GRID_W, H, hd)
    vg = v.reshape(B, rows, GRID_W, H, hd)
    rpb_f = rpb.astype(jnp.float32)

    def one_row(args):
        r, qr = args
        rs = jnp.clip(r - kr // 2, 0, rows - kr)
        k_rows = lax.dynamic_slice_in_dim(kg, rs, kr, axis=1)
        v_rows = lax.dynamic_slice_in_dim(vg, rs, kr, axis=1)
        kw = k_rows[:, :, kcol]
        vw = v_rows[:, :, kcol]
        dr_idx = rs + jnp.arange(kr) - r + (NA_ROWS - 1)
        bias = rpb_f[:, dr_idx[None, None, :, None], dc_idx[:, :, None, :]]
        s = jnp.einsum('bjqhd,bvjuhd->bhjqvu', qr, kw).astype(jnp.float32) * scale + bias[None]
        s = jnp.where(mask, s, NEG_INF)
        shp = s.shape
        p = jax.nn.softmax(s.reshape(shp[:-2] + (kr * kc,)), axis=-1).reshape(shp).astype(v.dtype)
        return jnp.einsum('bhjqvu,bvjuhd->bjqhd', p, vw)

    o = lax.map(one_row, (jnp.arange(rows), qg))
    return jnp.moveaxis(o, 0, 1).reshape(B, T, H * hd)


def mixer_ab(h, w_in, w_out, q_norm_a, k_norm_a, q_norm_b, k_norm_b, sink_b):
    B, T, _ = h.shape
    widths = [A_HEADS * HEAD_DIM, A_KV_HEADS * HEAD_DIM, A_KV_HEADS * HEAD_DIM,
              B_HEADS * HEAD_DIM, B_KV_HEADS * HEAD_DIM, B_KV_HEADS * HEAD_DIM]
    qa, ka, va, qb, kb, vb = split_cols(h @ w_in, widths)
    qa = axial_rope(rms_norm(qa.reshape(B, T, A_HEADS, HEAD_DIM), q_norm_a))
    ka = axial_rope(rms_norm(ka.reshape(B, T, A_KV_HEADS, HEAD_DIM), k_norm_a))
    va = va.reshape(B, T, A_KV_HEADS, HEAD_DIM)
    qb = rms_norm(qb.reshape(B, T, B_HEADS, HEAD_DIM), q_norm_b)
    kb = rms_norm(kb.reshape(B, T, B_KV_HEADS, HEAD_DIM), k_norm_b)
    vb = vb.reshape(B, T, B_KV_HEADS, HEAD_DIM)
    oa = global_attention(qa, ka, va)
    ob = window_sink_attention(qb, kb, vb, sink_b, alibi_slopes(B_HEADS))
    return jnp.concatenate([oa, ob], axis=-1) @ w_out


def mixer_c(h, w_in, w_out, q_norm_c, k_norm_c, rpb_c):
    B, T, _ = h.shape
    w = C_HEADS * HEAD_DIM
    q, k, v = split_cols(h @ w_in, [w, w, w])
    q = rms_norm(q.reshape(B, T, C_HEADS, HEAD_DIM), q_norm_c)
    k = rms_norm(k.reshape(B, T, C_HEADS, HEAD_DIM), k_norm_c)
    v = v.reshape(B, T, C_HEADS, HEAD_DIM)
    return neighborhood_attention(q, k, v, rpb_c) @ w_out


def expert_choice_ffn(h, w_router, w_gate, w_up, w_down):
    N, D = h.shape
    cap = EC_CAPACITY_FACTOR * N // N_EXPERTS
    aff = jax.nn.softmax((h @ w_router).astype(jnp.float32), axis=-1)
    gate, idx = lax.top_k(aff.T, cap)

    def expert(args):
        ids, a, b, c = args
        xi = h[ids]
        return (jax.nn.silu(xi @ a) * (xi @ b)) @ c

    ye = lax.map(expert, (idx, w_gate, w_up, w_down))
    ye = ye * gate[..., None].astype(ye.dtype)
    return jnp.zeros_like(h).at[idx.reshape(-1)].add(ye.reshape(-1, D))


def trunk(x, attn_norm, ffn_norm, w_in_ab, w_out_ab, q_norm_a, k_norm_a, q_norm_b, k_norm_b, sink_b,
          w_in_c, w_out_c, q_norm_c, k_norm_c, rpb_c, w_router, w_gate, w_up, w_down):
    B, T, D = x.shape
    for layer in range(DEPTH):
        h = rms_norm(x, attn_norm[layer])
        if layer % 2 == 0:
            e = layer // 2
            x = x + mixer_ab(h, w_in_ab[e], w_out_ab[e], q_norm_a[e], k_norm_a[e],
                             q_norm_b[e], k_norm_b[e], sink_b[e])
        else:
            o = layer // 2
            x = x + mixer_c(h, w_in_c[o], w_out_c[o], q_norm_c[o], k_norm_c[o], rpb_c[o])
        h = rms_norm(x, ffn_norm[layer]).reshape(B * T, D)
        x = x + expert_choice_ffn(h, w_router[layer], w_gate[layer], w_up[layer],
                                  w_down[layer]).reshape(B, T, D)
    return x


def setup_inputs(seed: int = 0) -> dict:
    key = jax.random.key(seed)
    ks = jax.random.split(key, 21)
    f32 = jnp.float32

    def nrm(k, shape, scale):
        return jax.random.normal(k, shape, f32) * scale

    def gain(k, shape):
        return 1.0 + 0.02 * jax.random.normal(k, shape, f32)

    return {
        "x_prompt": nrm(ks[0], (BATCH, SEQ, D_MODEL), 1.0),
        "x_sample": nrm(ks[1], (DEC_BATCH, DEC_SEQ, D_MODEL), 1.0),
        "attn_norm": gain(ks[2], (DEPTH, D_MODEL)),
        "ffn_norm": gain(ks[3], (DEPTH, D_MODEL)),
        "w_in_ab": nrm(ks[4], (N_EVEN, D_MODEL, AB_WIDTH), D_MODEL ** -0.5),
        "w_out_ab": nrm(ks[5], (N_EVEN, N_HEADS * HEAD_DIM, D_MODEL), (N_HEADS * HEAD_DIM) ** -0.5),
        "q_norm_a": gain(ks[6], (N_EVEN, HEAD_DIM)),
        "k_norm_a": gain(ks[7], (N_EVEN, HEAD_DIM)),
        "q_norm_b": gain(ks[8], (N_EVEN, HEAD_DIM)),
        "k_norm_b": gain(ks[9], (N_EVEN, HEAD_DIM)),
        "sink_b": nrm(ks[10], (N_EVEN, B_HEADS), 1.0),
        "w_in_c": nrm(ks[11], (N_ODD, D_MODEL, C_WIDTH), D_MODEL ** -0.5),
        "w_out_c": nrm(ks[12], (N_ODD, C_HEADS * HEAD_DIM, D_MODEL), (C_HEADS * HEAD_DIM) ** -0.5),
        "q_norm_c": gain(ks[13], (N_ODD, HEAD_DIM)),
        "k_norm_c": gain(ks[14], (N_ODD, HEAD_DIM)),
        "rpb_c": nrm(ks[15], (N_ODD, C_HEADS, 2 * NA_ROWS - 1, 2 * NA_COLS - 1), 0.1),
        "w_router": nrm(ks[16], (DEPTH, D_MODEL, N_EXPERTS), D_MODEL ** -0.5),
        "w_gate": nrm(ks[17], (DEPTH, N_EXPERTS, D_MODEL, D_FF), D_MODEL ** -0.5),
        "w_up": nrm(ks[18], (DEPTH, N_EXPERTS, D_MODEL, D_FF), D_MODEL ** -0.5),
        "w_down": nrm(ks[19], (DEPTH, N_EXPERTS, D_FF, D_MODEL), D_FF ** -0.5),
    }


def reference(x_prompt, x_sample, attn_norm, ffn_norm, w_in_ab, w_out_ab, q_norm_a, k_norm_a,
              q_norm_b, k_norm_b, sink_b, w_in_c, w_out_c, q_norm_c, k_norm_c, rpb_c,
              w_router, w_gate, w_up, w_down):
    y_prompt = trunk(x_prompt, attn_norm, ffn_norm, w_in_ab, w_out_ab, q_norm_a, k_norm_a,
                     q_norm_b, k_norm_b, sink_b, w_in_c, w_out_c, q_norm_c, k_norm_c, rpb_c,
                     w_router, w_gate, w_up, w_down)
    y_sample = trunk(x_sample, attn_norm, ffn_norm, w_in_ab, w_out_ab, q_norm_a, k_norm_a,
                     q_norm_b, k_norm_b, sink_b, w_in_c, w_out_c, q_norm_c, k_norm_c, rpb_c,
                     w_router, w_gate, w_up, w_down)
    return (y_prompt, y_sample)
```

```python
import functools

import jax
import jax.numpy as jnp
import numpy as np
from jax import lax
from jax.experimental import pallas as pl
from jax.experimental.pallas import tpu as pltpu

F32 = jnp.float32
BF16 = jnp.bfloat16
I32 = jnp.int32

HEAD = 128
GRID_COLS = 64
QBLK = 128
N_EXP = 16
NORM_EPS = 1e-6
NEG_INF = -1e30
ROPE_THETA = 10000.0
NA_ROWS, NA_COLS = 8, 16
NBR_QROWS = 4
ROW_TOKENS = 128
VMEM_LIMIT = 56 * 1024 * 1024


def _cparams(sem):
    return pltpu.CompilerParams(dimension_semantics=sem, vmem_limit_bytes=VMEM_LIMIT)


def _rms(x, g):
    r = lax.rsqrt(jnp.mean(x * x, axis=-1, keepdims=True) + NORM_EPS)
    return (x * r) * g


def _in_ranges(j, ranges):
    ok = None
    for lo, hi in ranges:
        c = (j >= lo) & (j < hi)
        ok = c if ok is None else (ok | c)
    return ok


def _swap_halves(y):
    lane = lax.broadcasted_iota(I32, y.shape, 1)
    return jnp.where((lane % 64) < 32, pltpu.roll(y, 96, 1), pltpu.roll(y, 32, 1))


def _proj_kernel(*refs, rope_tiles, norm_tiles, plain_tiles, tn):
    if rope_tiles:
        x_ref, g_ref, w_ref, gain_ref, cos_ref, sin_ref, o_ref, xn_ref = refs
    else:
        x_ref, g_ref, w_ref, gain_ref, o_ref, xn_ref = refs
    j = pl.program_id(1)

    @pl.when(j == 0)
    def _():
        xn_ref[...] = _rms(x_ref[...], g_ref[...]).astype(BF16)

    def heads(acc):
        return [acc[:, h * HEAD:(h + 1) * HEAD] for h in range(tn // HEAD)]

    def normed(acc):
        return [_rms(y, gain_ref[:, h * HEAD:(h + 1) * HEAD]) for h, y in enumerate(heads(acc))]

    def store(ys):
        for h, y in enumerate(ys):
            o_ref[:, h * HEAD:(h + 1) * HEAD] = y.astype(o_ref.dtype)

    def matmul():
        return jnp.dot(xn_ref[...], w_ref[...], preferred_element_type=F32)

    if rope_tiles:
        @pl.when(_in_ranges(j, rope_tiles))
        def _():
            c, s = cos_ref[...], sin_ref[...]
            store([y * c + _swap_halves(y) * s for y in normed(matmul())])

    if norm_tiles:
        @pl.when(_in_ranges(j, norm_tiles))
        def _():
            store(normed(matmul()))

    @pl.when(_in_ranges(j, plain_tiles))
    def _():
        o_ref[...] = matmul().astype(o_ref.dtype)


def _rope_tables(t_max):
    quarter = HEAD // 4
    t = jnp.arange(t_max)
    inv = ROPE_THETA ** (-jnp.arange(quarter, dtype=F32) / quarter)
    ang_r = (t // GRID_COLS).astype(F32)[:, None] * inv
    ang_c = (t % GRID_COLS).astype(F32)[:, None] * inv
    cr, sr, cc, sc = jnp.cos(ang_r), jnp.sin(ang_r), jnp.cos(ang_c), jnp.sin(ang_c)
    return (jnp.concatenate([cr, cr, cc, cc], axis=-1),
            jnp.concatenate([-sr, sr, -sc, sc], axis=-1))


def _proj(x, g, w, gain, segs, *, rope_tiles, norm_tiles, plain_tiles, tm, tn):
    nt, d = x.shape
    nout = w.shape[1]
    (rows0, t0), (rows1, t1) = segs
    nb0 = rows0 // tm

    def pos_block(i):
        return jnp.where(i < nb0, i % (t0 // tm), (i - nb0) % (t1 // tm))

    in_specs = [
        pl.BlockSpec((tm, d), lambda i, j: (i, 0)),
        pl.BlockSpec((1, d), lambda i, j: (0, 0)),
        pl.BlockSpec((d, tn), lambda i, j: (0, j)),
        pl.BlockSpec((1, tn), lambda i, j: (0, j)),
    ]
    args = [x, g.reshape(1, d), w, gain.reshape(1, nout)]
    if rope_tiles:
        cos, sin = _rope_tables(max(t0, t1))
        in_specs += [pl.BlockSpec((tm, HEAD), lambda i, j: (pos_block(i), 0))] * 2
        args += [cos, sin]
    kern = functools.partial(_proj_kernel, rope_tiles=rope_tiles, norm_tiles=norm_tiles,
                             plain_tiles=plain_tiles, tn=tn)
    return pl.pallas_call(
        kern,
        grid=(nt // tm, nout // tn),
        in_specs=in_specs,
        out_specs=pl.BlockSpec((tm, tn), lambda i, j: (i, j)),
        out_shape=jax.ShapeDtypeStruct((nt, nout), BF16),
        scratch_shapes=[pltpu.VMEM((tm, d), BF16)],
        compiler_params=_cparams(("parallel", "arbitrary")),
        name="in_proj",
    )(*args)


def _stack_heads(q, n):
    return jnp.concatenate([q[:, g * HEAD:(g + 1) * HEAD] for g in range(n)], axis=0)


def _gattn_kernel(*refs, tq, tk, nkc, scale, group, aliased):
    if aliased:
        q_ref, k_ref, v_ref, _, o_ref, m_ref, l_ref, acc_ref = refs
    else:
        q_ref, k_ref, v_ref, o_ref, m_ref, l_ref, acc_ref = refs
    q4 = _stack_heads(q_ref[...], group)
    m_ref[...] = jnp.full(m_ref.shape, -jnp.inf, F32)
    l_ref[...] = jnp.zeros(l_ref.shape, F32)
    acc_ref[...] = jnp.zeros(acc_ref.shape, F32)

    def body(c, carry):
        start = pl.multiple_of(c * tk, tk)
        kc = k_ref[pl.ds(start, tk), :]
        vc = v_ref[pl.ds(start, tk), :]
        s = lax.dot_general(q4, kc, (((1,), (1,)), ((), ())), preferred_element_type=F32) * scale
        m_prev = m_ref[...]
        m_new = jnp.maximum(m_prev, jnp.max(s, axis=-1, keepdims=True))
        alpha = jnp.exp(m_prev - m_new)
        p = jnp.exp(s - m_new)
        l_ref[...] = alpha * l_ref[...] + jnp.sum(p, axis=-1, keepdims=True)
        acc_ref[...] = alpha * acc_ref[...] + jnp.dot(p.astype(BF16), vc, preferred_element_type=F32)
        m_ref[...] = m_new
        return carry

    lax.fori_loop(0, nkc, body, 0)
    out = acc_ref[...] * (1.0 / l_ref[...])
    for g in range(group):
        o_ref[:, g * HEAD:(g + 1) * HEAD] = out[g * tq:(g + 1) * tq].astype(o_ref.dtype)


def _gattn(qkv, o_prev, *, row_off, nseq, t, tq, tk, out_cols):
    nt = qkv.shape[0]
    group, kvh = 4, 2
    nq = t // tq
    qoff = row_off // tq
    soff = row_off // t
    qw = group * HEAD
    aliased = o_prev is not None
    in_specs = [
        pl.BlockSpec((tq, qw), lambda b, h, i: (qoff + b * nq + i, h)),
        pl.BlockSpec((t, HEAD), lambda b, h, i: (soff + b, 8 + h)),
        pl.BlockSpec((t, HEAD), lambda b, h, i: (soff + b, 10 + h)),
    ]
    args = [qkv, qkv, qkv]
    if aliased:
        in_specs.append(pl.BlockSpec(memory_space=pl.ANY))
        args.append(o_prev)
    kern = functools.partial(_gattn_kernel, tq=tq, tk=tk, nkc=t // tk, scale=HEAD ** -0.5,
                             group=group, aliased=aliased)
    return pl.pallas_call(
        kern,
        grid=(nseq, kvh, nq),
        in_specs=in_specs,
        out_specs=pl.BlockSpec((tq, qw), lambda b, h, i: (qoff + b * nq + i, h)),
        out_shape=jax.ShapeDtypeStruct((nt, out_cols), BF16),
        scratch_shapes=[pltpu.VMEM((group * tq, 1), F32), pltpu.VMEM((group * tq, 1), F32),
                        pltpu.VMEM((group * tq, HEAD), F32)],
        input_output_aliases={3: 0} if aliased else {},
        compiler_params=_cparams(("parallel", "parallel", "arbitrary")),
        name="global_attn",
    )(*args)


def _wattn_kernel(slope_ref, sink_ref, q_ref, kp_ref, kc_ref, kn_ref, vp_ref, vc_ref, vn_ref, _, o_ref,
                  *, nb, scale, group):
    kh = pl.program_id(1)
    i = pl.program_id(2)
    q4 = _stack_heads(q_ref[...], group)
    kcat = jnp.concatenate([kp_ref[...], kc_ref[...], kn_ref[...]], axis=0)
    vcat = jnp.concatenate([vp_ref[...], vc_ref[...], vn_ref[...]], axis=0)
    rows, span = group * QBLK, 3 * QBLK
    s = lax.dot_general(q4, kcat, (((1,), (1,)), ((), ())), preferred_element_type=F32) * scale
    a = lax.broadcasted_iota(I32, (rows, span), 0) % QBLK
    jj = lax.broadcasted_iota(I32, (rows, span), 1)
    rel = a + QBLK - jj
    dist = jnp.abs(rel)
    ok = (dist <= QBLK) & ((jj >= QBLK) | (i > 0)) & ((jj < 2 * QBLK) | (i < nb - 1))
    rgrp = lax.broadcasted_iota(I32, (rows, 1), 0) // QBLK
    slope = jnp.zeros((rows, 1), F32)
    sink = jnp.zeros((rows, 1), F32)
    for g in range(group):
        slope = jnp.where(rgrp == g, slope_ref[kh * group + g], slope)
        sink = jnp.where(rgrp == g, sink_ref[kh * group + g], sink)
    s = s + (-slope) * dist.astype(F32)
    s = jnp.where(ok, s, NEG_INF)
    m = jnp.maximum(jnp.max(s, axis=-1, keepdims=True), sink)
    p = jnp.exp(s - m)
    denom = jnp.sum(p, axis=-1, keepdims=True) + jnp.exp(sink - m)
    p = p * (1.0 / denom)
    out = jnp.dot(p.astype(BF16), vcat, preferred_element_type=F32)
    for g in range(group):
        o_ref[:, g * HEAD:(g + 1) * HEAD] = out[g * QBLK:(g + 1) * QBLK].astype(o_ref.dtype)


def _wattn(qkv, slopes, sink, o_prev, *, row_off, nseq, t):
    nt = qkv.shape[0]
    group, kvh = 4, 2
    nb = t // QBLK
    off = row_off // QBLK
    qw = group * HEAD

    def kv_spec(col0, d):
        return pl.BlockSpec(
            (QBLK, HEAD), lambda b, h, i: (off + b * nb + jnp.clip(i + d, 0, nb - 1), col0 + h))

    smem = pl.BlockSpec(memory_space=pltpu.SMEM)
    kern = functools.partial(_wattn_kernel, nb=nb, scale=HEAD ** -0.5, group=group)
    return pl.pallas_call(
        kern,
        grid=(nseq, kvh, nb),
        in_specs=[smem, smem,
                  pl.BlockSpec((QBLK, qw), lambda b, h, i: (off + b * nb + i, 3 + h)),
                  kv_spec(20, -1), kv_spec(20, 0), kv_spec(20, 1),
                  kv_spec(22, -1), kv_spec(22, 0), kv_spec(22, 1),
                  pl.BlockSpec(memory_space=pl.ANY)],
        out_specs=pl.BlockSpec((QBLK, qw), lambda b, h, i: (off + b * nb + i, 2 + h)),
        out_shape=jax.ShapeDtypeStruct(o_prev.shape, o_prev.dtype),
        input_output_aliases={9: 0},
        compiler_params=_cparams(("parallel", "parallel", "arbitrary")),
        name="window_attn",
    )(slopes, sink, qkv, qkv, qkv, qkv, qkv, qkv, qkv, o_prev)


def _nattn_kernel(*refs, scale, aliased):
    if aliased:
        q_ref, kp_ref, kc_ref, kn_ref, vp_ref, vc_ref, vn_ref, b_ref, _, o_ref = refs
    else:
        q_ref, kp_ref, kc_ref, kn_ref, vp_ref, vc_ref, vn_ref, b_ref, o_ref = refs
    kcat = jnp.concatenate([kp_ref[...], kc_ref[...], kn_ref[...]], axis=0)
    vcat = jnp.concatenate([vp_ref[...], vc_ref[...], vn_ref[...]], axis=0)
    s = lax.dot_general(q_ref[...], kcat, (((1,), (1,)), ((), ())), preferred_element_type=F32)
    bias = b_ref[0]
    s = jnp.where(bias > 0.5 * NEG_INF, s * scale + bias, NEG_INF)
    m = jnp.max(s, axis=-1, keepdims=True)
    p = jnp.exp(s - m)
    l = jnp.sum(p, axis=-1, keepdims=True)
    out = jnp.dot(p.astype(BF16), vcat, preferred_element_type=F32) * (1.0 / l)
    o_ref[...] = out.astype(o_ref.dtype)


def _nbr_bias_table(rpb):
    qr_n, rows = NBR_QROWS, 3 * NBR_QROWS
    blk = np.arange(3)[:, None, None]
    r = blk * qr_n + np.arange(qr_n)[None, :, None]
    kr = (blk - 1) * qr_n + np.arange(3 * qr_n)[None, None, :]
    rs = np.clip(r - NA_ROWS // 2, 0, rows - NA_ROWS)
    row_ok = (kr >= rs) & (kr < rs + NA_ROWS)
    dr = np.clip(kr - r + (NA_ROWS - 1), 0, 2 * NA_ROWS - 2)
    qc = np.arange(GRID_COLS)[:, None]
    kc = np.arange(GRID_COLS)[None, :]
    cs = np.clip(qc - NA_COLS // 2, 0, GRID_COLS - NA_COLS)
    col_ok = (kc >= cs) & (kc < cs + NA_COLS)
    dc = np.clip(kc - qc, -(NA_COLS - 1), NA_COLS - 1) + (NA_COLS - 1)
    ok = row_ok[:, :, None, :, None] & col_ok[None, None, :, None, :]
    dr5 = np.broadcast_to(dr[:, :, None, :, None], ok.shape)
    dc5 = np.broadcast_to(dc[None, None, :, None, :], ok.shape)
    bias = rpb.astype(F32)[:, dr5, dc5]
    bias = jnp.where(ok[None], bias, NEG_INF)
    h = rpb.shape[0]
    nq, nk = qr_n * GRID_COLS, 3 * qr_n * GRID_COLS
    return jnp.swapaxes(bias, 0, 1).reshape(3 * h, nq, nk)


def _nattn(qkv, bias_tab, o_prev, *, row_off, nseq, t, nheads):
    nt = qkv.shape[0]
    tq = NBR_QROWS * GRID_COLS
    nj = t // tq
    assert nj >= 3
    off = row_off // tq
    aliased = o_prev is not None

    def kv_spec(col0, d):
        return pl.BlockSpec(
            (tq, HEAD), lambda b, h, j: (off + b * nj + jnp.clip(j + d, 0, nj - 1), col0 + h))

    def bias_idx(b, h, j):
        var = jnp.where(j == 0, 0, jnp.where(j == nj - 1, 2, 1))
        return (var * nheads + h, 0, 0)

    in_specs = [pl.BlockSpec((tq, HEAD), lambda b, h, j: (off + b * nj + j, h)),
                kv_spec(nheads, -1), kv_spec(nheads, 0), kv_spec(nheads, 1),
                kv_spec(2 * nheads, -1), kv_spec(2 * nheads, 0), kv_spec(2 * nheads, 1),
                pl.BlockSpec((1, tq, 3 * tq), bias_idx)]
    args = [qkv] * 7 + [bias_tab]
    if aliased:
        in_specs.append(pl.BlockSpec(memory_space=pl.ANY))
        args.append(o_prev)
    kern = functools.partial(_nattn_kernel, scale=HEAD ** -0.5, aliased=aliased)
    return pl.pallas_call(
        kern,
        grid=(nseq, nheads, nj),
        in_specs=in_specs,
        out_specs=pl.BlockSpec((tq, HEAD), lambda b, h, j: (off + b * nj + j, h)),
        out_shape=jax.ShapeDtypeStruct((nt, nheads * HEAD), BF16),
        input_output_aliases={8: 0} if aliased else {},
        compiler_params=_cparams(("parallel", "parallel", "arbitrary")),
        name="nbr_attn",
    )(*args)


def _outproj_kernel(a_ref, w_ref, x_ref, o_ref):
    o_ref[...] = x_ref[...] + jnp.dot(a_ref[...], w_ref[...], preferred_element_type=F32)


def _outproj(a, w, x, *, tm):
    nt, d = x.shape
    k = a.shape[1]
    return pl.pallas_call(
        _outproj_kernel,
        grid=(nt // tm,),
        in_specs=[pl.BlockSpec((tm, k), lambda i: (i, 0)),
                  pl.BlockSpec((k, d), lambda i: (0, 0)),
                  pl.BlockSpec((tm, d), lambda i: (i, 0))],
        out_specs=pl.BlockSpec((tm, d), lambda i: (i, 0)),
        out_shape=jax.ShapeDtypeStruct((nt, d), F32),
        compiler_params=_cparams(("parallel",)),
        name="out_proj",
    )(a, w, x)


def _router_kernel(x_ref, g_ref, wr_ref, h_ref, aff_ref):
    h = _rms(x_ref[...], g_ref[...])
    h_ref[...] = h
    logits = lax.dot_general(wr_ref[...], h.astype(BF16), (((1,), (1,)), ((), ())),
                             preferred_element_type=F32)
    m = jnp.max(logits, axis=0, keepdims=True)
    p = jnp.exp(logits - m)
    aff_ref[...] = p / jnp.sum(p, axis=0, keepdims=True)


def _router(x, g, wr_t, *, tm):
    nt, d = x.shape
    e = wr_t.shape[0]
    return pl.pallas_call(
        _router_kernel,
        grid=(nt // tm,),
        in_specs=[pl.BlockSpec((tm, d), lambda i: (i, 0)),
                  pl.BlockSpec((1, d), lambda i: (0, 0)),
                  pl.BlockSpec((e, d), lambda i: (0, 0))],
        out_specs=[pl.BlockSpec((tm, d), lambda i: (i, 0)),
                   pl.BlockSpec((e, tm), lambda i: (0, i))],
        out_shape=[jax.ShapeDtypeStruct((nt, d), F32), jax.ShapeDtypeStruct((e, nt), F32)],
        compiler_params=_cparams(("parallel",)),
        name="router",
    )(x, g.reshape(1, d), wr_t)


def _split3(a):
    hi = a.astype(BF16)
    r1 = a - hi.astype(F32)
    mid = r1.astype(BF16)
    lo = (r1 - mid.astype(F32)).astype(BF16)
    return hi, mid, lo


def _select_kernel(a_ref, idx_ref, gate_ref, *, cap, nrows):
    a = a_ref[0]
    shape = a.shape
    ones_col = jnp.ones((nrows, ROW_TOKENS), BF16)

    def ind(mask):
        return jnp.where(mask, 1.0, 0.0)

    def count(mask):
        return jnp.sum(jnp.sum(ind(mask), axis=0, keepdims=True), axis=1, keepdims=True)

    def step(_, st):
        lo, hi, found, tf = st
        mid = lo + ((hi - lo) >> 1)
        thr = lax.bitcast_convert_type(mid, F32)
        c = count(a >= thr)
        ge = c >= cap
        hit = (c == cap) & (found == 0)
        return (jnp.where(ge, mid, lo), jnp.where(ge, hi, mid),
                jnp.where(hit, 1, found), jnp.where(hit, mid, tf))

    z = jnp.zeros((1, 1), I32)
    lo, hi, found, tf = lax.fori_loop(0, 31, step, (z, z + 0x7F800000, z, z))
    v = lax.bitcast_convert_type(lo, F32)
    gt = a > v
    eq = a == v
    need = cap - count(gt)
    sub = lax.broadcasted_iota(I32, (ROW_TOKENS, ROW_TOKENS), 0)
    lane = lax.broadcasted_iota(I32, (ROW_TOKENS, ROW_TOKENS), 1)
    l_incl = ind(lane <= sub).astype(BF16)
    rr = lax.broadcasted_iota(I32, (nrows, nrows), 0)
    rc = lax.broadcasted_iota(I32, (nrows, nrows), 1)
    u_strict = ind(rr < rc).astype(BF16)

    def prefix(mask_bf):
        cin = jnp.dot(l_incl, mask_bf, preferred_element_type=F32)
        tot = jnp.broadcast_to(cin[ROW_TOKENS - 1:ROW_TOKENS, :], (8, nrows))
        cex = jnp.dot(tot.astype(BF16), u_strict, preferred_element_type=F32)[0:1, :]
        return cin, cin[ROW_TOKENS - 1:ROW_TOKENS, :], cex

    eq_f = ind(eq)
    ceq, _, ceq_off = prefix(eq_f.astype(BF16))
    eq_rank = ceq - eq_f + ceq_off
    sel_tie = ind(gt | (eq & (eq_rank < need)))
    sel = jnp.where(found > 0, ind(a >= lax.bitcast_convert_type(tf, F32)), sel_tie)

    cin, tot, cex = prefix(sel.astype(BF16))
    cinc = cex + tot
    pos = lax.broadcasted_iota(I32, (cap, nrows), 0).astype(F32)
    before = cinc <= pos
    before_bf = jnp.where(before, 1.0, 0.0).astype(BF16)
    rstar = jnp.dot(before_bf, ones_col, preferred_element_type=F32)
    skipped = jnp.dot(jnp.where(before, tot, 0.0).astype(BF16), ones_col, preferred_element_type=F32)
    pe = lax.broadcasted_iota(I32, (cap, ROW_TOKENS), 0).astype(F32) - skipped
    rid = lax.broadcasted_iota(I32, (cap, nrows), 1).astype(F32)
    rowsel = jnp.where(rid == rstar[:, 0:1], 1.0, 0.0).astype(BF16)
    nt_dims = (((1,), (1,)), ((), ()))
    cin_g = lax.dot_general(rowsel, cin.astype(BF16), nt_dims, preferred_element_type=F32)
    lstar = jnp.dot(jnp.where(cin_g <= pe, 1.0, 0.0).astype(BF16),
                    jnp.ones((ROW_TOKENS, ROW_TOKENS), BF16), preferred_element_type=F32)
    idx_ref[0] = (rstar * ROW_TOKENS + lstar).astype(I32)
    a_g = sum(lax.dot_general(rowsel, part, nt_dims, preferred_element_type=F32) for part in _split3(a))
    lid = lax.broadcasted_iota(I32, (cap, ROW_TOKENS), 1).astype(F32)
    gate = jnp.sum(jnp.where(lid == lstar, a_g, 0.0), axis=1, keepdims=True)
    gate_ref[0] = jnp.broadcast_to(gate, (cap, ROW_TOKENS))


def _select(aff_t, *, cap):
    e, n = aff_t.shape
    r = n // ROW_TOKENS
    nrows = max(r, ROW_TOKENS)
    a2 = jnp.swapaxes(aff_t.reshape(e, r, ROW_TOKENS), 1, 2)
    if nrows > r:
        a2 = jnp.pad(a2, ((0, 0), (0, 0), (0, nrows - r)), constant_values=-1.0)
    kern = functools.partial(_select_kernel, cap=cap, nrows=nrows)
    return pl.pallas_call(
        kern,
        grid=(e,),
        in_specs=[pl.BlockSpec((1, ROW_TOKENS, nrows), lambda i: (i, 0, 0))],
        out_specs=[pl.BlockSpec((1, cap, ROW_TOKENS), lambda i: (i, 0, 0)),
                   pl.BlockSpec((1, cap, ROW_TOKENS), lambda i: (i, 0, 0))],
        out_shape=[jax.ShapeDtypeStruct((e, cap, ROW_TOKENS), I32),
                   jax.ShapeDtypeStruct((e, cap, ROW_TOKENS), F32)],
        compiler_params=_cparams(("parallel",)),
        name="expert_select",
    )(a2)


def _row_copy(src_hbm, dst_vmem, sem, token, slot):
    return pltpu.make_async_copy(src_hbm.at[pl.ds(token, 1)], dst_vmem.at[pl.ds(slot, 1)], sem)


def _gather_kernel(idx_ref, h_hbm, o_ref, buf, sem, *, tg):
    base = pl.program_id(0) * tg

    def start(s, c):
        _row_copy(h_hbm, buf, sem, idx_ref[base + s], s).start()
        return c

    def wait(s, c):
        _row_copy(h_hbm, buf, sem, idx_ref[base + s], s).wait()
        return c

    lax.fori_loop(0, tg, start, 0)
    lax.fori_loop(0, tg, wait, 0)
    o_ref[...] = buf[...].astype(o_ref.dtype)


def _gather(h, idx_flat, *, tg):
    nt, d = h.shape
    n = idx_flat.shape[0]
    kern = functools.partial(_gather_kernel, tg=tg)
    return pl.pallas_call(
        kern,
        grid_spec=pltpu.PrefetchScalarGridSpec(
            num_scalar_prefetch=1,
            grid=(n // tg,),
            in_specs=[pl.BlockSpec(memory_space=pl.ANY)],
            out_specs=pl.BlockSpec((tg, d), lambda i, idx: (i, 0)),
            scratch_shapes=[pltpu.VMEM((tg, d), F32), pltpu.SemaphoreType.DMA(())],
        ),
        out_shape=jax.ShapeDtypeStruct((n, d), BF16),
        compiler_params=_cparams(("arbitrary",)),
        name="token_gather",
    )(idx_flat, h)


def _ffn_kernel(x_ref, gate_ref, wg_ref, wu_ref, wd_ref, o_ref, *, nf):
    f = pl.program_id(2)
    x = x_ref[0]
    g = jnp.dot(x, wg_ref[0, 0].astype(BF16), preferred_element_type=F32)
    u = jnp.dot(x, wu_ref[0, 0].astype(BF16), preferred_element_type=F32)
    a = (g * jax.nn.sigmoid(g)) * u
    y = jnp.dot(a.astype(BF16), wd_ref[0, 0].astype(BF16), preferred_element_type=F32)

    @pl.when(f == 0)
    def _():
        o_ref[0] = y

    @pl.when(f > 0)
    def _():
        o_ref[0] += y

    @pl.when(f == nf - 1)
    def _():
        o_ref[0] = o_ref[0] * gate_ref[0][:, 0:1]


def _ffn(xg, gate, w_gate, w_up, w_down, layer, *, tm, tf):
    e, capt, d = xg.shape
    dff = w_gate.shape[-1]
    nf = dff // tf
    kern = functools.partial(_ffn_kernel, nf=nf)
    return pl.pallas_call(
        kern,
        grid=(e, capt // tm, nf),
        in_specs=[pl.BlockSpec((1, tm, d), lambda ei, i, f: (ei, i, 0)),
                  pl.BlockSpec((1, tm, ROW_TOKENS), lambda ei, i, f: (ei, i, 0)),
                  pl.BlockSpec((1, 1, d, tf), lambda ei, i, f: (layer, ei, 0, f)),
                  pl.BlockSpec((1, 1, d, tf), lambda ei, i, f: (layer, ei, 0, f)),
                  pl.BlockSpec((1, 1, tf, d), lambda ei, i, f: (layer, ei, f, 0))],
        out_specs=pl.BlockSpec((1, tm, d), lambda ei, i, f: (ei, i, 0)),
        out_shape=jax.ShapeDtypeStruct((e, capt, d), F32),
        compiler_params=_cparams(("parallel", "parallel", "arbitrary")),
        name="expert_ffn",
    )(xg, gate, w_gate, w_up, w_down)


def _combine_kernel(idx_ref, y_ref, x_hbm, o_hbm, buf, sem_in, sem_out, *, tg):
    del x_hbm
    base = pl.program_id(0) * tg

    def rd(s):
        return _row_copy(o_hbm, buf, sem_in, idx_ref[base + s], s)

    def wr(s):
        t = idx_ref[base + s]
        return pltpu.make_async_copy(buf.at[pl.ds(s, 1)], o_hbm.at[pl.ds(t, 1)], sem_out)

    def loop(fn):
        def body(s, c):
            fn(s)
            return c
        lax.fori_loop(0, tg, body, 0)

    loop(lambda s: rd(s).start())
    loop(lambda s: rd(s).wait())
    buf[...] = buf[...] + y_ref[...]
    loop(lambda s: wr(s).start())
    loop(lambda s: wr(s).wait())


def _combine(x, yeg, idx_flat, *, tg):
    nt, d = x.shape
    n = idx_flat.shape[0]
    kern = functools.partial(_combine_kernel, tg=tg)
    return pl.pallas_call(
        kern,
        grid_spec=pltpu.PrefetchScalarGridSpec(
            num_scalar_prefetch=1,
            grid=(n // tg,),
            in_specs=[pl.BlockSpec((tg, d), lambda i, idx: (i, 0)),
                      pl.BlockSpec(memory_space=pl.ANY)],
            out_specs=pl.BlockSpec(memory_space=pl.ANY),
            scratch_shapes=[pltpu.VMEM((tg, d), F32), pltpu.SemaphoreType.DMA(()),
                            pltpu.SemaphoreType.DMA(())],
        ),
        out_shape=jax.ShapeDtypeStruct((nt, d), F32),
        input_output_aliases={2: 0},
        compiler_params=_cparams(("arbitrary",)),
        name="expert_combine",
    )(idx_flat, yeg, x)


def _moe(x, segs, g, w_router_l, w_gate, w_up, w_down, layer):
    nt, d = x.shape
    h, aff_t = _router(x, g, jnp.swapaxes(w_router_l, 0, 1).astype(BF16), tm=512)
    ids, gates = [], []
    row = 0
    for rows, _ in segs:
        cap = 2 * rows // N_EXP
        idx, gate = _select(aff_t[:, row:row + rows], cap=cap)
        ids.append(idx[:, :, 0] + row)
        gates.append(gate)
        row += rows
    idx_all = jnp.concatenate(ids, axis=1)
    gate_all = jnp.concatenate(gates, axis=1)
    capt = idx_all.shape[1]
    idx_flat = idx_all.reshape(-1)
    tg = 256
    xg = _gather(h, idx_flat, tg=tg).reshape(N_EXP, capt, d)
    tm = 1024 if capt % 1024 == 0 else capt
    yeg = _ffn(xg, gate_all, w_gate, w_up, w_down, layer, tm=tm, tf=256 if w_gate.shape[-1] % 256 == 0 else w_gate.shape[-1])
    return _combine(x, yeg.reshape(N_EXP * capt, d), idx_flat, tg=tg)


def _trunk(x, segs, attn_norm, ffn_norm, w_in_ab, w_out_ab, q_norm_a, k_norm_a, q_norm_b, k_norm_b, sink_b,
           w_in_c, w_out_c, q_norm_c, k_norm_c, rpb_c, w_router, w_gate, w_up, w_down):
    depth = attn_norm.shape[0]
    ones = jnp.ones((HEAD,), F32)
    tm_proj = 1024
    for layer in range(depth):
        if layer % 2 == 0:
            e = layer // 2
            gain = jnp.concatenate([jnp.tile(q_norm_a[e], 8), jnp.tile(k_norm_a[e], 2), jnp.tile(ones, 2),
                                    jnp.tile(q_norm_b[e], 8), jnp.tile(k_norm_b[e], 2), jnp.tile(ones, 2)])
            qkv = _proj(x, attn_norm[layer], w_in_ab[e].astype(BF16), gain, segs,
                        rope_tiles=((0, 5),), norm_tiles=((6, 11),), plain_tiles=((5, 6), (11, 12)),
                        tm=tm_proj, tn=256)
            o = None
            row = 0
            for rows, t in segs:
                o = _gattn(qkv, o, row_off=row, nseq=rows // t, t=t, tq=min(256, t), tk=min(512, t),
                           out_cols=16 * HEAD)
                row += rows
            slopes = 2.0 ** (-8.0 * jnp.arange(1, 9, dtype=F32) / 8)
            row = 0
            for rows, t in segs:
                o = _wattn(qkv, slopes, sink_b[e].astype(F32), o, row_off=row, nseq=rows // t, t=t)
                row += rows
            x = _outproj(o, w_out_ab[e].astype(BF16), x, tm=512)
        else:
            c = layer // 2
            gain = jnp.concatenate([jnp.tile(q_norm_c[c], 16), jnp.tile(k_norm_c[c], 16), jnp.tile(ones, 16)])
            qkv = _proj(x, attn_norm[layer], w_in_c[c].astype(BF16), gain, segs,
                        rope_tiles=(), norm_tiles=((0, 16),), plain_tiles=((16, 24),),
                        tm=tm_proj, tn=256)
            tab = _nbr_bias_table(rpb_c[c])
            o = None
            row = 0
            for rows, t in segs:
                o = _nattn(qkv, tab, o, row_off=row, nseq=rows // t, t=t, nheads=16)
                row += rows
            x = _outproj(o, w_out_c[c].astype(BF16), x, tm=512)
        x = _moe(x, segs, ffn_norm[layer], w_router[layer], w_gate, w_up, w_down, layer)
    return x


def kernel(x_prompt, x_sample, attn_norm, ffn_norm, w_in_ab, w_out_ab, q_norm_a, k_norm_a, q_norm_b, k_norm_b,
           sink_b, w_in_c, w_out_c, q_norm_c, k_norm_c, rpb_c, w_router, w_gate, w_up, w_down):
    bp, tp, d = x_prompt.shape
    bs, ts, _ = x_sample.shape
    x = jnp.concatenate([x_sample.reshape(bs * ts, d), x_prompt.reshape(bp * tp, d)], axis=0)
    segs = ((bs * ts, ts), (bp * tp, tp))
    y = _trunk(x, segs, attn_norm, ffn_norm, w_in_ab, w_out_ab, q_norm_a, k_norm_a, q_norm_b, k_norm_b,
               sink_b, w_in_c, w_out_c, q_norm_c, k_norm_c, rpb_c, w_router, w_gate, w_up, w_down)
    y_sample = y[:bs * ts].reshape(bs, ts, d)
    y_prompt = y[bs * ts:].reshape(bp, tp, d)
    return (y_prompt, y_sample)
```

```python
import functools

import jax
import jax.numpy as jnp
import numpy as np
from jax import lax
from jax.experimental import pallas as pl
from jax.experimental.pallas import tpu as pltpu

F32 = jnp.float32
BF16 = jnp.bfloat16
I32 = jnp.int32

HEAD = 128
GRID_COLS = 64
QBLK = 128
N_EXP = 16
NORM_EPS = 1e-6
NEG_INF = -1e30
ROPE_THETA = 10000.0
NA_ROWS, NA_COLS = 8, 16
NBR_QROWS = 4
NBR_HEADS_PER_STEP = 4
LOG2E = float(np.log2(np.e))
DMA_ISSUE_UNROLL = 8
GATTN_TQ, GATTN_TK = 128, 2048
ROW_TOKENS = 128
VMEM_LIMIT = 56 * 1024 * 1024


def _cparams(sem):
    return pltpu.CompilerParams(dimension_semantics=sem, vmem_limit_bytes=VMEM_LIMIT)


def _rms(x, g):
    r = lax.rsqrt(jnp.mean(x * x, axis=-1, keepdims=True) + NORM_EPS)
    return (x * r) * g


def _in_ranges(j, ranges):
    ok = None
    for lo, hi in ranges:
        c = (j >= lo) & (j < hi)
        ok = c if ok is None else (ok | c)
    return ok


def _swap_halves(y):
    lane = lax.broadcasted_iota(I32, y.shape, 1)
    return jnp.where((lane % 64) < 32, pltpu.roll(y, 96, 1), pltpu.roll(y, 32, 1))


def _proj_kernel(*refs, rope_tiles, norm_tiles, plain_tiles, tn):
    if rope_tiles:
        x_ref, g_ref, w_ref, gain_ref, cos_ref, sin_ref, o_ref, xn_ref = refs
    else:
        x_ref, g_ref, w_ref, gain_ref, o_ref, xn_ref = refs
    j = pl.program_id(1)

    @pl.when(j == 0)
    def _():
        xn_ref[...] = _rms(x_ref[...], g_ref[...]).astype(BF16)

    def heads(acc):
        return [acc[:, h * HEAD:(h + 1) * HEAD] for h in range(tn // HEAD)]

    def normed(acc):
        return [_rms(y, gain_ref[:, h * HEAD:(h + 1) * HEAD]) for h, y in enumerate(heads(acc))]

    def store(ys):
        for h, y in enumerate(ys):
            o_ref[:, h * HEAD:(h + 1) * HEAD] = y.astype(o_ref.dtype)

    def matmul():
        return jnp.dot(xn_ref[...], w_ref[...], preferred_element_type=F32)

    if rope_tiles:
        @pl.when(_in_ranges(j, rope_tiles))
        def _():
            c, s = cos_ref[...], sin_ref[...]
            store([y * c + _swap_halves(y) * s for y in normed(matmul())])

    if norm_tiles:
        @pl.when(_in_ranges(j, norm_tiles))
        def _():
            store(normed(matmul()))

    @pl.when(_in_ranges(j, plain_tiles))
    def _():
        o_ref[...] = matmul().astype(o_ref.dtype)


def _rope_tables(t_max):
    quarter = HEAD // 4
    t = jnp.arange(t_max)
    inv = ROPE_THETA ** (-jnp.arange(quarter, dtype=F32) / quarter)
    ang_r = (t // GRID_COLS).astype(F32)[:, None] * inv
    ang_c = (t % GRID_COLS).astype(F32)[:, None] * inv
    cr, sr, cc, sc = jnp.cos(ang_r), jnp.sin(ang_r), jnp.cos(ang_c), jnp.sin(ang_c)
    return (jnp.concatenate([cr, cr, cc, cc], axis=-1),
            jnp.concatenate([-sr, sr, -sc, sc], axis=-1))


def _proj(x, g, w, gain, segs, *, rope_tiles, norm_tiles, plain_tiles, tm, tn):
    nt, d = x.shape
    nout = w.shape[1]
    (rows0, t0), (rows1, t1) = segs
    nb0 = rows0 // tm

    def pos_block(i):
        return jnp.where(i < nb0, i % (t0 // tm), (i - nb0) % (t1 // tm))

    in_specs = [
        pl.BlockSpec((tm, d), lambda i, j: (i, 0)),
        pl.BlockSpec((1, d), lambda i, j: (0, 0)),
        pl.BlockSpec((d, tn), lambda i, j: (0, j)),
        pl.BlockSpec((1, tn), lambda i, j: (0, j)),
    ]
    args = [x, g.reshape(1, d), w, gain.reshape(1, nout)]
    if rope_tiles:
        cos, sin = _rope_tables(max(t0, t1))
        in_specs += [pl.BlockSpec((tm, HEAD), lambda i, j: (pos_block(i), 0))] * 2
        args += [cos, sin]
    kern = functools.partial(_proj_kernel, rope_tiles=rope_tiles, norm_tiles=norm_tiles,
                             plain_tiles=plain_tiles, tn=tn)
    return pl.pallas_call(
        kern,
        grid=(nt // tm, nout // tn),
        in_specs=in_specs,
        out_specs=pl.BlockSpec((tm, tn), lambda i, j: (i, j)),
        out_shape=jax.ShapeDtypeStruct((nt, nout), BF16),
        scratch_shapes=[pltpu.VMEM((tm, d), BF16)],
        compiler_params=_cparams(("parallel", "arbitrary")),
        name="in_proj",
    )(*args)


def _stack_heads(q, n):
    return jnp.concatenate([q[:, g * HEAD:(g + 1) * HEAD] for g in range(n)], axis=0)


def _gattn_kernel(q_ref, k_ref, v_ref, o_ref, s0_ref, s1_ref, m_ref, l_ref, acc_ref, *, tq, tk, nkc, group):
    c2 = (HEAD ** -0.5) * float(np.log2(np.e))
    q4 = _stack_heads(q_ref[...], group)
    m_ref[...] = jnp.full(m_ref.shape, -jnp.inf, F32)
    l_ref[...] = jnp.zeros(l_ref.shape, F32)
    acc_ref[...] = jnp.zeros(acc_ref.shape, F32)
    nlane = tk // HEAD

    def qk(c, dst_ref):
        kc = k_ref[pl.ds(pl.multiple_of(c * tk, tk), tk), :]
        dst_ref[...] = lax.dot_general(q4, kc, (((1,), (1,)), ((), ())), preferred_element_type=F32)

    def softmax_pv(c, src_ref):
        s = src_ref[...]
        lane_max = functools.reduce(jnp.maximum, [s[:, j * HEAD:(j + 1) * HEAD] for j in range(nlane)])
        m_prev = m_ref[...]
        m_new = jnp.maximum(m_prev, jnp.max(lane_max, axis=-1, keepdims=True) * c2)
        alpha = jnp.exp2(m_prev - m_new)
        p = jnp.exp2(s * c2 - m_new)
        lane_sum = functools.reduce(jnp.add, [p[:, j * HEAD:(j + 1) * HEAD] for j in range(nlane)])
        l_ref[...] = alpha * l_ref[...] + lane_sum
        vc = v_ref[pl.ds(pl.multiple_of(c * tk, tk), tk), :]
        acc_ref[...] = alpha * acc_ref[...] + jnp.dot(p.astype(BF16), vc, preferred_element_type=F32)
        m_ref[...] = m_new

    qk(0, s0_ref)

    def pair(c, carry):
        qk(2 * c + 1, s1_ref)
        softmax_pv(2 * c, s0_ref)
        qk(2 * c + 2, s0_ref)
        softmax_pv(2 * c + 1, s1_ref)
        return carry

    lax.fori_loop(0, nkc // 2 - 1, pair, 0)
    qk(nkc - 1, s1_ref)
    softmax_pv(nkc - 2, s0_ref)
    softmax_pv(nkc - 1, s1_ref)
    out = acc_ref[...] * (1.0 / jnp.sum(l_ref[...], axis=-1, keepdims=True))
    for g in range(group):
        o_ref[:, g * HEAD:(g + 1) * HEAD] = out[g * tq:(g + 1) * tq].astype(o_ref.dtype)


def _gattn(qkv, *, row_off, nseq, t, tq, tk):
    group, kvh = 4, 2
    nq = t // tq
    nkc = t // tk
    assert nkc % 2 == 0
    qoff = row_off // tq
    soff = row_off // t
    qw = group * HEAD
    kern = functools.partial(_gattn_kernel, tq=tq, tk=tk, nkc=nkc, group=group)
    return pl.pallas_call(
        kern,
        grid=(nseq, kvh, nq),
        in_specs=[pl.BlockSpec((tq, qw), lambda b, h, i: (qoff + b * nq + i, h)),
                  pl.BlockSpec((t, HEAD), lambda b, h, i: (soff + b, 8 + h)),
                  pl.BlockSpec((t, HEAD), lambda b, h, i: (soff + b, 10 + h))],
        out_specs=pl.BlockSpec((tq, qw), lambda b, h, i: (b * nq + i, h)),
        out_shape=jax.ShapeDtypeStruct((nseq * t, kvh * qw), BF16),
        scratch_shapes=[pltpu.VMEM((group * tq, tk), F32), pltpu.VMEM((group * tq, tk), F32),
                        pltpu.VMEM((group * tq, 1), F32), pltpu.VMEM((group * tq, HEAD), F32),
                        pltpu.VMEM((group * tq, HEAD), F32)],
        compiler_params=_cparams(("parallel", "parallel", "arbitrary")),
        name="global_attn",
    )(qkv, qkv, qkv)


def _seq_block(b, blocks):
    (n0, s0), (_, s1) = blocks
    first = b < n0
    i = jnp.where(first, b % s0, (b - n0) % s1)
    return i, jnp.where(first, s0, s1)


def _neighbour_block(b, d, blocks):
    i, n = _seq_block(b, blocks)
    return b - i + jnp.clip(i + d, 0, n - 1)


def _wattn_kernel(slope_ref, sink_ref, q_ref, kp_ref, kc_ref, kn_ref, vp_ref, vc_ref, vn_ref, o_ref,
                  *, blocks, scale, group):
    kh = pl.program_id(0)
    i, nb = _seq_block(pl.program_id(1), blocks)
    q4 = _stack_heads(q_ref[...], group)
    kcat = jnp.concatenate([kp_ref[...], kc_ref[...], kn_ref[...]], axis=0)
    vcat = jnp.concatenate([vp_ref[...], vc_ref[...], vn_ref[...]], axis=0)
    rows, span = group * QBLK, 3 * QBLK
    s = lax.dot_general(q4, kcat, (((1,), (1,)), ((), ())), preferred_element_type=F32) * scale
    a = lax.broadcasted_iota(I32, (rows, span), 0) % QBLK
    jj = lax.broadcasted_iota(I32, (rows, span), 1)
    rel = a + QBLK - jj
    dist = jnp.abs(rel)
    ok = (dist <= QBLK) & ((jj >= QBLK) | (i > 0)) & ((jj < 2 * QBLK) | (i < nb - 1))
    rgrp = lax.broadcasted_iota(I32, (rows, 1), 0) // QBLK
    slope = jnp.zeros((rows, 1), F32)
    sink = jnp.zeros((rows, 1), F32)
    for g in range(group):
        slope = jnp.where(rgrp == g, slope_ref[kh * group + g], slope)
        sink = jnp.where(rgrp == g, sink_ref[kh * group + g], sink)
    s = s + (-slope) * dist.astype(F32)
    s = jnp.where(ok, s, NEG_INF)
    m = jnp.maximum(jnp.max(s, axis=-1, keepdims=True), sink)
    p = jnp.exp(s - m)
    denom = jnp.sum(p, axis=-1, keepdims=True) + jnp.exp(sink - m)
    p = p * (1.0 / denom)
    out = jnp.dot(p.astype(BF16), vcat, preferred_element_type=F32)
    for g in range(group):
        o_ref[:, g * HEAD:(g + 1) * HEAD] = out[g * QBLK:(g + 1) * QBLK].astype(o_ref.dtype)


def _wattn(qkv, slopes, sink, segs):
    nt = qkv.shape[0]
    group, kvh = 4, 2
    qw = group * HEAD
    blocks = tuple((rows // QBLK, t // QBLK) for rows, t in segs)

    def kv_spec(col0, d):
        return pl.BlockSpec((QBLK, HEAD), lambda h, b: (_neighbour_block(b, d, blocks), col0 + h))

    smem = pl.BlockSpec(memory_space=pltpu.SMEM)
    kern = functools.partial(_wattn_kernel, blocks=blocks, scale=HEAD ** -0.5, group=group)
    return pl.pallas_call(
        kern,
        grid=(kvh, nt // QBLK),
        in_specs=[smem, smem,
                  pl.BlockSpec((QBLK, qw), lambda h, b: (b, 3 + h)),
                  kv_spec(20, -1), kv_spec(20, 0), kv_spec(20, 1),
                  kv_spec(22, -1), kv_spec(22, 0), kv_spec(22, 1)],
        out_specs=pl.BlockSpec((QBLK, qw), lambda h, b: (b, h)),
        out_shape=jax.ShapeDtypeStruct((nt, kvh * qw), BF16),
        compiler_params=_cparams(("parallel", "arbitrary")),
        name="window_attn",
    )(slopes, sink, qkv, qkv, qkv, qkv, qkv, qkv, qkv)


def _nattn_kernel(q_ref, kp_ref, kc_ref, kn_ref, vp_ref, vc_ref, vn_ref, b_ref, o_ref, *, c2, nh):
    for h in range(nh):
        cols = slice(h * HEAD, (h + 1) * HEAD)
        kcat = jnp.concatenate([kp_ref[:, cols], kc_ref[:, cols], kn_ref[:, cols]], axis=0)
        vcat = jnp.concatenate([vp_ref[:, cols], vc_ref[:, cols], vn_ref[:, cols]], axis=0)
        s = lax.dot_general(q_ref[:, cols], kcat, (((1,), (1,)), ((), ())), preferred_element_type=F32)
        s = s * c2 + b_ref[h]
        nlane = s.shape[1] // HEAD
        lane_max = functools.reduce(jnp.maximum, [s[:, j * HEAD:(j + 1) * HEAD] for j in range(nlane)])
        p = jnp.exp2(s - jnp.max(lane_max, axis=-1, keepdims=True))
        lane_sum = functools.reduce(jnp.add, [p[:, j * HEAD:(j + 1) * HEAD] for j in range(nlane)])
        l = jnp.sum(lane_sum, axis=-1, keepdims=True)
        out = jnp.dot(p.astype(BF16), vcat, preferred_element_type=F32) * (1.0 / l)
        o_ref[:, cols] = out.astype(o_ref.dtype)


def _nbr_bias_table(rpb):
    qr_n, rows = NBR_QROWS, 3 * NBR_QROWS
    blk = np.arange(3)[:, None, None]
    r = blk * qr_n + np.arange(qr_n)[None, :, None]
    kr = (blk - 1) * qr_n + np.arange(3 * qr_n)[None, None, :]
    rs = np.clip(r - NA_ROWS // 2, 0, rows - NA_ROWS)
    row_ok = (kr >= rs) & (kr < rs + NA_ROWS)
    dr = np.clip(kr - r + (NA_ROWS - 1), 0, 2 * NA_ROWS - 2)
    qc = np.arange(GRID_COLS)[:, None]
    kc = np.arange(GRID_COLS)[None, :]
    cs = np.clip(qc - NA_COLS // 2, 0, GRID_COLS - NA_COLS)
    col_ok = (kc >= cs) & (kc < cs + NA_COLS)
    dc = np.clip(kc - qc, -(NA_COLS - 1), NA_COLS - 1) + (NA_COLS - 1)
    h = rpb.shape[0]
    by_col = jnp.take(rpb.astype(F32), jnp.asarray(dc.reshape(-1)), axis=2)
    by_row = jnp.take(by_col, jnp.asarray(dr.reshape(-1)), axis=1)
    bias = by_row.reshape(h, 3, qr_n, 3 * qr_n, GRID_COLS, GRID_COLS)
    ok = row_ok[None, :, :, :, None, None] & col_ok[None, None, None, None, :, :]
    bias = jnp.where(ok, bias * LOG2E, NEG_INF)
    bias = jnp.transpose(bias, (1, 0, 2, 4, 3, 5))
    return bias.reshape(3, h, qr_n * GRID_COLS, 3 * qr_n * GRID_COLS)


def _nattn(qkv, bias_tab, segs, *, nheads):
    nt = qkv.shape[0]
    tq = NBR_QROWS * GRID_COLS
    nh = NBR_HEADS_PER_STEP
    hw = nh * HEAD
    ng = nheads // nh
    blocks = tuple((rows // tq, t // tq) for rows, t in segs)
    assert all(per_seq >= 3 for _, per_seq in blocks)

    def kv_spec(g0, d):
        return pl.BlockSpec((tq, hw), lambda g, b: (_neighbour_block(b, d, blocks), g0 + g))

    def bias_idx(g, b):
        j, nj = _seq_block(b, blocks)
        return (jnp.where(j == 0, 0, jnp.where(j == nj - 1, 2, 1)), g, 0, 0)

    kern = functools.partial(_nattn_kernel, c2=(HEAD ** -0.5) * LOG2E, nh=nh)
    return pl.pallas_call(
        kern,
        grid=(ng, nt // tq),
        in_specs=[pl.BlockSpec((tq, hw), lambda g, b: (b, g)),
                  kv_spec(ng, -1), kv_spec(ng, 0), kv_spec(ng, 1),
                  kv_spec(2 * ng, -1), kv_spec(2 * ng, 0), kv_spec(2 * ng, 1),
                  pl.BlockSpec((None, nh, tq, 3 * tq), bias_idx)],
        out_specs=pl.BlockSpec((tq, hw), lambda g, b: (b, g)),
        out_shape=jax.ShapeDtypeStruct((nt, nheads * HEAD), BF16),
        compiler_params=_cparams(("parallel", "arbitrary")),
        name="nbr_attn",
    )(*([qkv] * 7 + [bias_tab]))


def _outproj_kernel(*refs):
    *aw, x_ref, o_ref = refs
    n = len(aw) // 2
    y = x_ref[...]
    for a_ref, w_ref in zip(aw[:n], aw[n:]):
        y = y + jnp.dot(a_ref[...], w_ref[...], preferred_element_type=F32)
    o_ref[...] = y


def _outproj(parts, weights, x, *, tm):
    nt, d = x.shape
    in_specs = ([pl.BlockSpec((tm, a.shape[1]), lambda i: (i, 0)) for a in parts]
                + [pl.BlockSpec(w.shape, lambda i: (0, 0)) for w in weights]
                + [pl.BlockSpec((tm, d), lambda i: (i, 0))])
    return pl.pallas_call(
        _outproj_kernel,
        grid=(nt // tm,),
        in_specs=in_specs,
        out_specs=pl.BlockSpec((tm, d), lambda i: (i, 0)),
        out_shape=jax.ShapeDtypeStruct((nt, d), F32),
        compiler_params=_cparams(("parallel",)),
        name="out_proj",
    )(*parts, *weights, x)


def _router_kernel(x_ref, g_ref, wr_ref, h_ref, aff_ref):
    h = _rms(x_ref[...], g_ref[...])
    h_ref[...] = h
    logits = lax.dot_general(wr_ref[...], h.astype(BF16), (((1,), (1,)), ((), ())),
                             preferred_element_type=F32)
    m = jnp.max(logits, axis=0, keepdims=True)
    p = jnp.exp(logits - m)
    aff_ref[...] = p / jnp.sum(p, axis=0, keepdims=True)


def _router(x, g, wr_t, *, tm):
    nt, d = x.shape
    e = wr_t.shape[0]
    return pl.pallas_call(
        _router_kernel,
        grid=(nt // tm,),
        in_specs=[pl.BlockSpec((tm, d), lambda i: (i, 0)),
                  pl.BlockSpec((1, d), lambda i: (0, 0)),
                  pl.BlockSpec((e, d), lambda i: (0, 0))],
        out_specs=[pl.BlockSpec((tm, d), lambda i: (i, 0)),
                   pl.BlockSpec((e, tm), lambda i: (0, i))],
        out_shape=[jax.ShapeDtypeStruct((nt, d), F32), jax.ShapeDtypeStruct((e, nt), F32)],
        compiler_params=_cparams(("parallel",)),
        name="router",
    )(x, g.reshape(1, d), wr_t)


def _split3(a):
    hi = a.astype(BF16)
    r1 = a - hi.astype(F32)
    mid = r1.astype(BF16)
    lo = (r1 - mid.astype(F32)).astype(BF16)
    return hi, mid, lo


def _select_kernel(a_ref, idx_ref, gate_ref, *, cap, nrows):
    a = a_ref[0]
    shape = a.shape
    ones_col = jnp.ones((nrows, ROW_TOKENS), BF16)

    def ind(mask):
        return jnp.where(mask, 1.0, 0.0)

    def count(mask):
        return jnp.sum(jnp.sum(ind(mask), axis=0, keepdims=True), axis=1, keepdims=True)

    def step(_, st):
        lo, hi, found, tf = st
        mid = lo + ((hi - lo) >> 1)
        thr = lax.bitcast_convert_type(mid, F32)
        c = count(a >= thr)
        ge = c >= cap
        hit = (c == cap) & (found == 0)
        return (jnp.where(ge, mid, lo), jnp.where(ge, hi, mid),
                jnp.where(hit, 1, found), jnp.where(hit, mid, tf))

    z = jnp.zeros((1, 1), I32)
    lo, hi, found, tf = lax.fori_loop(0, 31, step, (z, z + 0x7F800000, z, z))
    v = lax.bitcast_convert_type(lo, F32)
    gt = a > v
    eq = a == v
    need = cap - count(gt)
    sub = lax.broadcasted_iota(I32, (ROW_TOKENS, ROW_TOKENS), 0)
    lane = lax.broadcasted_iota(I32, (ROW_TOKENS, ROW_TOKENS), 1)
    l_incl = ind(lane <= sub).astype(BF16)
    rr = lax.broadcasted_iota(I32, (nrows, nrows), 0)
    rc = lax.broadcasted_iota(I32, (nrows, nrows), 1)
    u_strict = ind(rr < rc).astype(BF16)

    def prefix(mask_bf):
        cin = jnp.dot(l_incl, mask_bf, preferred_element_type=F32)
        tot = jnp.broadcast_to(cin[ROW_TOKENS - 1:ROW_TOKENS, :], (8, nrows))
        cex = jnp.dot(tot.astype(BF16), u_strict, preferred_element_type=F32)[0:1, :]
        return cin, cin[ROW_TOKENS - 1:ROW_TOKENS, :], cex

    eq_f = ind(eq)
    ceq, _, ceq_off = prefix(eq_f.astype(BF16))
    eq_rank = ceq - eq_f + ceq_off
    sel_tie = ind(gt | (eq & (eq_rank < need)))
    sel = jnp.where(found > 0, ind(a >= lax.bitcast_convert_type(tf, F32)), sel_tie)

    cin, tot, cex = prefix(sel.astype(BF16))
    cinc = cex + tot
    pos = lax.broadcasted_iota(I32, (cap, nrows), 0).astype(F32)
    before = cinc <= pos
    before_bf = jnp.where(before, 1.0, 0.0).astype(BF16)
    rstar = jnp.dot(before_bf, ones_col, preferred_element_type=F32)
    skipped = jnp.dot(jnp.where(before, tot, 0.0).astype(BF16), ones_col, preferred_element_type=F32)
    pe = lax.broadcasted_iota(I32, (cap, ROW_TOKENS), 0).astype(F32) - skipped
    rid = lax.broadcasted_iota(I32, (cap, nrows), 1).astype(F32)
    rowsel = jnp.where(rid == rstar[:, 0:1], 1.0, 0.0).astype(BF16)
    nt_dims = (((1,), (1,)), ((), ()))
    cin_g = lax.dot_general(rowsel, cin.astype(BF16), nt_dims, preferred_element_type=F32)
    lstar = jnp.dot(jnp.where(cin_g <= pe, 1.0, 0.0).astype(BF16),
                    jnp.ones((ROW_TOKENS, ROW_TOKENS), BF16), preferred_element_type=F32)
    idx_ref[0] = (rstar * ROW_TOKENS + lstar).astype(I32)
    a_g = sum(lax.dot_general(rowsel, part, nt_dims, preferred_element_type=F32) for part in _split3(a))
    lid = lax.broadcasted_iota(I32, (cap, ROW_TOKENS), 1).astype(F32)
    gate = jnp.sum(jnp.where(lid == lstar, a_g, 0.0), axis=1, keepdims=True)
    gate_ref[0] = jnp.broadcast_to(gate, (cap, ROW_TOKENS))


def _select(aff_t, *, cap):
    e, n = aff_t.shape
    r = n // ROW_TOKENS
    nrows = max(r, ROW_TOKENS)
    a2 = jnp.swapaxes(aff_t.reshape(e, r, ROW_TOKENS), 1, 2)
    if nrows > r:
        a2 = jnp.pad(a2, ((0, 0), (0, 0), (0, nrows - r)), constant_values=-1.0)
    kern = functools.partial(_select_kernel, cap=cap, nrows=nrows)
    return pl.pallas_call(
        kern,
        grid=(e,),
        in_specs=[pl.BlockSpec((1, ROW_TOKENS, nrows), lambda i: (i, 0, 0))],
        out_specs=[pl.BlockSpec((1, cap, ROW_TOKENS), lambda i: (i, 0, 0)),
                   pl.BlockSpec((1, cap, ROW_TOKENS), lambda i: (i, 0, 0))],
        out_shape=[jax.ShapeDtypeStruct((e, cap, ROW_TOKENS), I32),
                   jax.ShapeDtypeStruct((e, cap, ROW_TOKENS), F32)],
        compiler_params=_cparams(("parallel",)),
        name="expert_select",
    )(a2)


def _row_copy(src_hbm, dst_vmem, sem, token, slot):
    return pltpu.make_async_copy(src_hbm.at[pl.ds(token, 1)], dst_vmem.at[pl.ds(slot, 1)], sem)


def _gather_kernel(idx_ref, h_hbm, o_ref, buf, sem, *, tg):
    base = pl.program_id(0) * tg

    def start(s, c):
        _row_copy(h_hbm, buf, sem, idx_ref[base + s], s).start()
        return c

    def wait(s, c):
        _row_copy(h_hbm, buf, sem, idx_ref[base + s], s).wait()
        return c

    lax.fori_loop(0, tg, start, 0, unroll=DMA_ISSUE_UNROLL)
    lax.fori_loop(0, tg, wait, 0, unroll=DMA_ISSUE_UNROLL)
    o_ref[...] = buf[...].astype(o_ref.dtype)


def _gather(h, idx_flat, *, tg):
    nt, d = h.shape
    n = idx_flat.shape[0]
    kern = functools.partial(_gather_kernel, tg=tg)
    return pl.pallas_call(
        kern,
        grid_spec=pltpu.PrefetchScalarGridSpec(
            num_scalar_prefetch=1,
            grid=(n // tg,),
            in_specs=[pl.BlockSpec(memory_space=pl.ANY)],
            out_specs=pl.BlockSpec((tg, d), lambda i, idx: (i, 0)),
            scratch_shapes=[pltpu.VMEM((tg, d), F32), pltpu.SemaphoreType.DMA(())],
        ),
        out_shape=jax.ShapeDtypeStruct((n, d), BF16),
        compiler_params=_cparams(("arbitrary",)),
        name="token_gather",
    )(idx_flat, h)


def _ffn_kernel(x_ref, gate_ref, wg_ref, wu_ref, wd_ref, o_ref, *, nf):
    f = pl.program_id(2)
    x = x_ref[0]
    g = jnp.dot(x, wg_ref[0, 0].astype(BF16), preferred_element_type=F32)
    u = jnp.dot(x, wu_ref[0, 0].astype(BF16), preferred_element_type=F32)
    a = (g * jax.nn.sigmoid(g)) * u
    y = jnp.dot(a.astype(BF16), wd_ref[0, 0].astype(BF16), preferred_element_type=F32)

    @pl.when(f == 0)
    def _():
        o_ref[0] = y

    @pl.when(f > 0)
    def _():
        o_ref[0] += y

    @pl.when(f == nf - 1)
    def _():
        o_ref[0] = o_ref[0] * gate_ref[0][:, 0:1]


def _ffn(xg, gate, w_gate, w_up, w_down, layer, *, tm, tf):
    e, capt, d = xg.shape
    dff = w_gate.shape[-1]
    nf = dff // tf
    kern = functools.partial(_ffn_kernel, nf=nf)
    return pl.pallas_call(
        kern,
        grid=(e, capt // tm, nf),
        in_specs=[pl.BlockSpec((1, tm, d), lambda ei, i, f: (ei, i, 0)),
                  pl.BlockSpec((1, tm, ROW_TOKENS), lambda ei, i, f: (ei, i, 0)),
                  pl.BlockSpec((1, 1, d, tf), lambda ei, i, f: (layer, ei, 0, f)),
                  pl.BlockSpec((1, 1, d, tf), lambda ei, i, f: (layer, ei, 0, f)),
                  pl.BlockSpec((1, 1, tf, d), lambda ei, i, f: (layer, ei, f, 0))],
        out_specs=pl.BlockSpec((1, tm, d), lambda ei, i, f: (ei, i, 0)),
        out_shape=jax.ShapeDtypeStruct((e, capt, d), F32),
        compiler_params=_cparams(("parallel", "parallel", "arbitrary")),
        name="expert_ffn",
    )(xg, gate, w_gate, w_up, w_down)


def _combine_kernel(idx_ref, y_ref, x_hbm, o_hbm, buf, sem_in, sem_out, *, tg):
    del x_hbm
    base = pl.program_id(0) * tg

    def rd(s):
        return _row_copy(o_hbm, buf, sem_in, idx_ref[base + s], s)

    def wr(s):
        t = idx_ref[base + s]
        return pltpu.make_async_copy(buf.at[pl.ds(s, 1)], o_hbm.at[pl.ds(t, 1)], sem_out)

    def loop(fn):
        def body(s, c):
            fn(s)
            return c
        lax.fori_loop(0, tg, body, 0, unroll=DMA_ISSUE_UNROLL)

    loop(lambda s: rd(s).start())
    loop(lambda s: rd(s).wait())
    buf[...] = buf[...] + y_ref[...]
    loop(lambda s: wr(s).start())
    loop(lambda s: wr(s).wait())


def _combine(x, yeg, idx_flat, *, tg):
    nt, d = x.shape
    n = idx_flat.shape[0]
    kern = functools.partial(_combine_kernel, tg=tg)
    return pl.pallas_call(
        kern,
        grid_spec=pltpu.PrefetchScalarGridSpec(
            num_scalar_prefetch=1,
            grid=(n // tg,),
            in_specs=[pl.BlockSpec((tg, d), lambda i, idx: (i, 0)),
                      pl.BlockSpec(memory_space=pl.ANY)],
            out_specs=pl.BlockSpec(memory_space=pl.ANY),
            scratch_shapes=[pltpu.VMEM((tg, d), F32), pltpu.SemaphoreType.DMA(()),
                            pltpu.SemaphoreType.DMA(())],
        ),
        out_shape=jax.ShapeDtypeStruct((nt, d), F32),
        input_output_aliases={2: 0},
        compiler_params=_cparams(("arbitrary",)),
        name="expert_combine",
    )(idx_flat, yeg, x)


def _moe(x, segs, g, w_router_l, w_gate, w_up, w_down, layer):
    nt, d = x.shape
    h, aff_t = _router(x, g, jnp.swapaxes(w_router_l, 0, 1).astype(BF16), tm=512)
    ids, gates = [], []
    row = 0
    for rows, _ in segs:
        cap = 2 * rows // N_EXP
        idx, gate = _select(aff_t[:, row:row + rows], cap=cap)
        ids.append(idx[:, :, 0] + row)
        gates.append(gate)
        row += rows
    idx_all = jnp.concatenate(ids, axis=1)
    gate_all = jnp.concatenate(gates, axis=1)
    capt = idx_all.shape[1]
    idx_flat = idx_all.reshape(-1)
    tg = 256
    xg = _gather(h, idx_flat, tg=tg).reshape(N_EXP, capt, d)
    tm = 1024 if capt % 1024 == 0 else capt
    yeg = _ffn(xg, gate_all, w_gate, w_up, w_down, layer, tm=tm, tf=256 if w_gate.shape[-1] % 256 == 0 else w_gate.shape[-1])
    return _combine(x, yeg.reshape(N_EXP * capt, d), idx_flat, tg=tg)


def _trunk(x, segs, attn_norm, ffn_norm, w_in_ab, w_out_ab, q_norm_a, k_norm_a, q_norm_b, k_norm_b, sink_b,
           w_in_c, w_out_c, q_norm_c, k_norm_c, rpb_c, w_router, w_gate, w_up, w_down):
    depth = attn_norm.shape[0]
    ones = jnp.ones((HEAD,), F32)
    tm_proj = 1024
    for layer in range(depth):
        if layer % 2 == 0:
            e = layer // 2
            gain = jnp.concatenate([jnp.tile(q_norm_a[e], 8), jnp.tile(k_norm_a[e], 2), jnp.tile(ones, 2),
                                    jnp.tile(q_norm_b[e], 8), jnp.tile(k_norm_b[e], 2), jnp.tile(ones, 2)])
            qkv = _proj(x, attn_norm[layer], w_in_ab[e].astype(BF16), gain, segs,
                        rope_tiles=((0, 5),), norm_tiles=((6, 11),), plain_tiles=((5, 6), (11, 12)),
                        tm=tm_proj, tn=256)
            oa = []
            row = 0
            for rows, t in segs:
                oa.append(_gattn(qkv, row_off=row, nseq=rows // t, t=t, tq=min(GATTN_TQ, t),
                                 tk=min(GATTN_TK, t // 2)))
                row += rows
            oa = jnp.concatenate(oa, axis=0)
            slopes = 2.0 ** (-8.0 * jnp.arange(1, 9, dtype=F32) / 8)
            ob = _wattn(qkv, slopes, sink_b[e].astype(F32), segs)
            w_out = w_out_ab[e].astype(BF16)
            ka = oa.shape[1]
            x = _outproj([oa, ob], [w_out[:ka], w_out[ka:]], x, tm=512)
        else:
            c = layer // 2
            gain = jnp.concatenate([jnp.tile(q_norm_c[c], 16), jnp.tile(k_norm_c[c], 16), jnp.tile(ones, 16)])
            qkv = _proj(x, attn_norm[layer], w_in_c[c].astype(BF16), gain, segs,
                        rope_tiles=(), norm_tiles=((0, 16),), plain_tiles=((16, 24),),
                        tm=tm_proj, tn=256)
            tab = _nbr_bias_table(rpb_c[c])
            o = _nattn(qkv, tab, segs, nheads=16)
            x = _outproj([o], [w_out_c[c].astype(BF16)], x, tm=512)
        x = _moe(x, segs, ffn_norm[layer], w_router[layer], w_gate, w_up, w_down, layer)
    return x


def kernel(x_prompt, x_sample, attn_norm, ffn_norm, w_in_ab, w_out_ab, q_norm_a, k_norm_a, q_norm_b, k_norm_b,
           sink_b, w_in_c, w_out_c, q_norm_c, k_norm_c, rpb_c, w_router, w_gate, w_up, w_down):
    bp, tp, d = x_prompt.shape
    bs, ts, _ = x_sample.shape
    x = jnp.concatenate([x_sample.reshape(bs * ts, d), x_prompt.reshape(bp * tp, d)], axis=0)
    segs = ((bs * ts, ts), (bp * tp, tp))
    y = _trunk(x, segs, attn_norm, ffn_norm, w_in_ab, w_out_ab, q_norm_a, k_norm_a, q_norm_b, k_norm_b,
               sink_b, w_in_c, w_out_c, q_norm_c, k_norm_c, rpb_c, w_router, w_gate, w_up, w_down)
    y_sample = y[:bs * ts].reshape(bs, ts, d)
    y_prompt = y[bs * ts:].reshape(bp, tp, d)
    return (y_prompt, y_sample)
```

```python
import functools

import jax
import jax.numpy as jnp
import numpy as np
from jax import lax
from jax.experimental import pallas as pl
from jax.experimental.pallas import tpu as pltpu

F32 = jnp.float32
BF16 = jnp.bfloat16
I32 = jnp.int32

HEAD = 128
GRID_COLS = 64
QBLK = 128
N_EXP = 16
NORM_EPS = 1e-6
NEG_INF = -1e30
ROPE_THETA = 10000.0
NA_ROWS, NA_COLS = 8, 16
NBR_QROWS = 4
NBR_HEADS_PER_STEP = 4
LOG2E = float(np.log2(np.e))
GATTN_TQ, GATTN_TK = 128, 2048
GATTN_ROWS = 64
ROW_TOKENS = 128
VMEM_LIMIT = 56 * 1024 * 1024


def _cparams(sem):
    return pltpu.CompilerParams(dimension_semantics=sem, vmem_limit_bytes=VMEM_LIMIT)


def _rms(x, g):
    r = lax.rsqrt(jnp.mean(x * x, axis=-1, keepdims=True) + NORM_EPS)
    return (x * r) * g


def _swap_halves(y):
    lane = lax.broadcasted_iota(I32, y.shape, 1)
    return jnp.where((lane % 64) < 32, pltpu.roll(y, 96, 1), pltpu.roll(y, 32, 1))


def _proj_kernel(*refs, tile_kinds, has_rope):
    if has_rope:
        x_ref, g_ref, w_ref, gain_ref, cos_ref, sin_ref, o_ref, xn_ref = refs
    else:
        x_ref, g_ref, w_ref, gain_ref, o_ref, xn_ref = refs
    j = pl.program_id(1)

    @pl.when(j == 0)
    def _():
        xn_ref[...] = _rms(x_ref[...], g_ref[...]).astype(BF16)

    def tile(kinds):
        acc = jnp.dot(xn_ref[...], w_ref[...], preferred_element_type=F32)
        for h, kind in enumerate(kinds):
            cols = slice(h * HEAD, (h + 1) * HEAD)
            y = acc[:, cols]
            if kind in "rn":
                y = _rms(y, gain_ref[:, cols])
            if kind == "r":
                y = y * cos_ref[...] + _swap_halves(y) * sin_ref[...]
            o_ref[:, cols] = y.astype(o_ref.dtype)

    for (lo, hi), kinds in tile_kinds:
        pl.when((j >= lo) & (j < hi))(functools.partial(tile, kinds))


def _rope_tables(t_max):
    quarter = HEAD // 4
    t = jnp.arange(t_max)
    inv = ROPE_THETA ** (-jnp.arange(quarter, dtype=F32) / quarter)
    ang_r = (t // GRID_COLS).astype(F32)[:, None] * inv
    ang_c = (t % GRID_COLS).astype(F32)[:, None] * inv
    cr, sr, cc, sc = jnp.cos(ang_r), jnp.sin(ang_r), jnp.cos(ang_c), jnp.sin(ang_c)
    return (jnp.concatenate([cr, cr, cc, cc], axis=-1),
            jnp.concatenate([-sr, sr, -sc, sc], axis=-1))


def _proj(x, g, w, gain, segs, *, tile_kinds, tm):
    nt, d = x.shape
    nout = w.shape[1]
    tn = HEAD * len(tile_kinds[0][1])
    assert tile_kinds[-1][0][1] * tn == nout
    rope_tiles = any("r" in kinds for _, kinds in tile_kinds)
    (rows0, t0), (rows1, t1) = segs
    nb0 = rows0 // tm

    def pos_block(i):
        return jnp.where(i < nb0, i % (t0 // tm), (i - nb0) % (t1 // tm))

    in_specs = [
        pl.BlockSpec((tm, d), lambda i, j: (i, 0)),
        pl.BlockSpec((1, d), lambda i, j: (0, 0)),
        pl.BlockSpec((d, tn), lambda i, j: (0, j)),
        pl.BlockSpec((1, tn), lambda i, j: (0, j)),
    ]
    args = [x, g.reshape(1, d), w, gain.reshape(1, nout)]
    if rope_tiles:
        cos, sin = _rope_tables(max(t0, t1))
        in_specs += [pl.BlockSpec((tm, HEAD), lambda i, j: (pos_block(i), 0))] * 2
        args += [cos, sin]
    kern = functools.partial(_proj_kernel, tile_kinds=tile_kinds, has_rope=rope_tiles)
    return pl.pallas_call(
        kern,
        grid=(nt // tm, nout // tn),
        in_specs=in_specs,
        out_specs=pl.BlockSpec((tm, tn), lambda i, j: (i, j)),
        out_shape=jax.ShapeDtypeStruct((nt, nout), BF16),
        scratch_shapes=[pltpu.VMEM((tm, d), BF16)],
        compiler_params=_cparams(("parallel", "arbitrary")),
        name="in_proj",
    )(*args)


def _tree_reduce(fn, xs):
    while len(xs) > 1:
        xs = [fn(xs[i], xs[i + 1]) for i in range(0, len(xs) - 1, 2)] + ([xs[-1]] if len(xs) % 2 else [])
    return xs[0]


def _stack_heads(q, n):
    return jnp.concatenate([q[:, g * HEAD:(g + 1) * HEAD] for g in range(n)], axis=0)


def _gattn_kernel(q_ref, k_ref, v_ref, o_ref, s0_ref, s1_ref, p_ref, m_ref, alpha_ref, l_ref, acc_ref,
                  *, tq, tk, nkc, group):
    c2 = (HEAD ** -0.5) * float(np.log2(np.e))
    q4 = _stack_heads(q_ref[...], group)
    m_ref[...] = jnp.full(m_ref.shape, -jnp.inf, F32)
    l_ref[...] = jnp.zeros(l_ref.shape, F32)
    acc_ref[...] = jnp.zeros(acc_ref.shape, F32)
    nlane = tk // HEAD

    def qk(c, dst_ref):
        kc = k_ref[pl.ds(pl.multiple_of(c * tk, tk), tk), :]
        dst_ref[...] = lax.dot_general(q4, kc, (((1,), (1,)), ((), ())), preferred_element_type=F32)

    def softmax_pv(c, src_ref):
        for r0 in range(0, group * tq, GATTN_ROWS):
            rows = slice(r0, r0 + GATTN_ROWS)
            lanes = [slice(j * HEAD, (j + 1) * HEAD) for j in range(nlane)]
            lane_max = _tree_reduce(jnp.maximum, [src_ref[rows, cols] for cols in lanes])
            m_prev = m_ref[rows, :]
            m_new = jnp.maximum(m_prev, jnp.max(lane_max, axis=-1, keepdims=True) * c2)
            alpha = jnp.exp2(m_prev - m_new)
            m_lanes = jnp.broadcast_to(m_new, (GATTN_ROWS, HEAD))
            ps = []
            for cols in lanes:
                p = jnp.exp2(src_ref[rows, cols] * c2 - m_lanes)
                p_ref[rows, cols] = p.astype(BF16)
                ps.append(p)
            l_ref[rows, :] = alpha * l_ref[rows, :] + _tree_reduce(jnp.add, ps)
            alpha_ref[rows, :] = alpha
            m_ref[rows, :] = m_new
        vc = v_ref[pl.ds(pl.multiple_of(c * tk, tk), tk), :]
        acc_ref[...] = alpha_ref[...] * acc_ref[...] + jnp.dot(p_ref[...], vc, preferred_element_type=F32)

    qk(0, s0_ref)

    def pair(c, carry):
        qk(2 * c + 1, s1_ref)
        softmax_pv(2 * c, s0_ref)
        qk(2 * c + 2, s0_ref)
        softmax_pv(2 * c + 1, s1_ref)
        return carry

    lax.fori_loop(0, nkc // 2 - 1, pair, 0)
    qk(nkc - 1, s1_ref)
    softmax_pv(nkc - 2, s0_ref)
    softmax_pv(nkc - 1, s1_ref)
    out = acc_ref[...] * (1.0 / jnp.sum(l_ref[...], axis=-1, keepdims=True))
    for g in range(group):
        o_ref[:, g * HEAD:(g + 1) * HEAD] = out[g * tq:(g + 1) * tq].astype(o_ref.dtype)


def _gattn(qkv, *, row_off, nseq, t, tq, tk):
    group, kvh = 4, 2
    nq = t // tq
    nkc = t // tk
    assert nkc % 2 == 0
    qoff = row_off // tq
    soff = row_off // t
    qw = group * HEAD
    kern = functools.partial(_gattn_kernel, tq=tq, tk=tk, nkc=nkc, group=group)
    return pl.pallas_call(
        kern,
        grid=(nseq, kvh, nq),
        in_specs=[pl.BlockSpec((tq, qw), lambda b, h, i: (qoff + b * nq + i, h)),
                  pl.BlockSpec((t, HEAD), lambda b, h, i: (soff + b, 8 + h)),
                  pl.BlockSpec((t, HEAD), lambda b, h, i: (soff + b, 10 + h))],
        out_specs=pl.BlockSpec((tq, qw), lambda b, h, i: (b * nq + i, h)),
        out_shape=jax.ShapeDtypeStruct((nseq * t, kvh * qw), BF16),
        scratch_shapes=[pltpu.VMEM((group * tq, tk), F32), pltpu.VMEM((group * tq, tk), F32),
                        pltpu.VMEM((group * tq, tk), BF16),
                        pltpu.VMEM((group * tq, 1), F32), pltpu.VMEM((group * tq, 1), F32),
                        pltpu.VMEM((group * tq, HEAD), F32), pltpu.VMEM((group * tq, HEAD), F32)],
        compiler_params=_cparams(("parallel", "parallel", "arbitrary")),
        name="global_attn",
    )(qkv, qkv, qkv)


def _seq_block(b, blocks):
    (n0, s0), (_, s1) = blocks
    first = b < n0
    i = jnp.where(first, b % s0, (b - n0) % s1)
    return i, jnp.where(first, s0, s1)


def _neighbour_block(b, d, blocks):
    i, n = _seq_block(b, blocks)
    return b - i + jnp.clip(i + d, 0, n - 1)


def _wattn_kernel(slope_ref, sink_ref, q_ref, kp_ref, kc_ref, kn_ref, vp_ref, vc_ref, vn_ref, o_ref,
                  *, blocks, scale, group):
    kh = pl.program_id(0)
    i, nb = _seq_block(pl.program_id(1), blocks)
    q4 = _stack_heads(q_ref[...], group)
    kcat = jnp.concatenate([kp_ref[...], kc_ref[...], kn_ref[...]], axis=0)
    vcat = jnp.concatenate([vp_ref[...], vc_ref[...], vn_ref[...]], axis=0)
    rows, span = group * QBLK, 3 * QBLK
    s = lax.dot_general(q4, kcat, (((1,), (1,)), ((), ())), preferred_element_type=F32) * scale
    a = lax.broadcasted_iota(I32, (rows, span), 0) % QBLK
    jj = lax.broadcasted_iota(I32, (rows, span), 1)
    rel = a + QBLK - jj
    dist = jnp.abs(rel)
    ok = (dist <= QBLK) & ((jj >= QBLK) | (i > 0)) & ((jj < 2 * QBLK) | (i < nb - 1))
    rgrp = lax.broadcasted_iota(I32, (rows, 1), 0) // QBLK
    slope = jnp.zeros((rows, 1), F32)
    sink = jnp.zeros((rows, 1), F32)
    for g in range(group):
        slope = jnp.where(rgrp == g, slope_ref[kh * group + g], slope)
        sink = jnp.where(rgrp == g, sink_ref[kh * group + g], sink)
    s = s + (-slope) * dist.astype(F32)
    s = jnp.where(ok, s, NEG_INF)
    m = jnp.maximum(jnp.max(s, axis=-1, keepdims=True), sink)
    p = jnp.exp(s - m)
    denom = jnp.sum(p, axis=-1, keepdims=True) + jnp.exp(sink - m)
    p = p * (1.0 / denom)
    out = jnp.dot(p.astype(BF16), vcat, preferred_element_type=F32)
    for g in range(group):
        o_ref[:, g * HEAD:(g + 1) * HEAD] = out[g * QBLK:(g + 1) * QBLK].astype(o_ref.dtype)


def _wattn(qkv, slopes, sink, segs):
    nt = qkv.shape[0]
    group, kvh = 4, 2
    qw = group * HEAD
    blocks = tuple((rows // QBLK, t // QBLK) for rows, t in segs)

    def kv_spec(col0, d):
        return pl.BlockSpec((QBLK, HEAD), lambda h, b: (_neighbour_block(b, d, blocks), col0 + h))

    smem = pl.BlockSpec(memory_space=pltpu.SMEM)
    kern = functools.partial(_wattn_kernel, blocks=blocks, scale=HEAD ** -0.5, group=group)
    return pl.pallas_call(
        kern,
        grid=(kvh, nt // QBLK),
        in_specs=[smem, smem,
                  pl.BlockSpec((QBLK, qw), lambda h, b: (b, 3 + h)),
                  kv_spec(20, -1), kv_spec(20, 0), kv_spec(20, 1),
                  kv_spec(22, -1), kv_spec(22, 0), kv_spec(22, 1)],
        out_specs=pl.BlockSpec((QBLK, qw), lambda h, b: (b, h)),
        out_shape=jax.ShapeDtypeStruct((nt, kvh * qw), BF16),
        compiler_params=_cparams(("parallel", "arbitrary")),
        name="window_attn",
    )(slopes, sink, qkv, qkv, qkv, qkv, qkv, qkv, qkv)


def _nattn_kernel(q_ref, kp_ref, kc_ref, kn_ref, vp_ref, vc_ref, vn_ref, b_ref, o_ref, *, c2, nh):
    for h in range(nh):
        cols = slice(h * HEAD, (h + 1) * HEAD)
        kcat = jnp.concatenate([kp_ref[:, cols], kc_ref[:, cols], kn_ref[:, cols]], axis=0)
        vcat = jnp.concatenate([vp_ref[:, cols], vc_ref[:, cols], vn_ref[:, cols]], axis=0)
        s = lax.dot_general(q_ref[:, cols], kcat, (((1,), (1,)), ((), ())), preferred_element_type=F32)
        s = s * c2 + b_ref[h]
        nlane = s.shape[1] // HEAD
        lane_max = functools.reduce(jnp.maximum, [s[:, j * HEAD:(j + 1) * HEAD] for j in range(nlane)])
        p = jnp.exp2(s - jnp.max(lane_max, axis=-1, keepdims=True))
        lane_sum = functools.reduce(jnp.add, [p[:, j * HEAD:(j + 1) * HEAD] for j in range(nlane)])
        l = jnp.sum(lane_sum, axis=-1, keepdims=True)
        out = jnp.dot(p.astype(BF16), vcat, preferred_element_type=F32) * (1.0 / l)
        o_ref[:, cols] = out.astype(o_ref.dtype)


def _nbr_bias_table(rpb):
    qr_n, rows = NBR_QROWS, 3 * NBR_QROWS
    blk = np.arange(3)[:, None, None]
    r = blk * qr_n + np.arange(qr_n)[None, :, None]
    kr = (blk - 1) * qr_n + np.arange(3 * qr_n)[None, None, :]
    rs = np.clip(r - NA_ROWS // 2, 0, rows - NA_ROWS)
    row_ok = (kr >= rs) & (kr < rs + NA_ROWS)
    dr = np.clip(kr - r + (NA_ROWS - 1), 0, 2 * NA_ROWS - 2)
    qc = np.arange(GRID_COLS)[:, None]
    kc = np.arange(GRID_COLS)[None, :]
    cs = np.clip(qc - NA_COLS // 2, 0, GRID_COLS - NA_COLS)
    col_ok = (kc >= cs) & (kc < cs + NA_COLS)
    dc = np.clip(kc - qc, -(NA_COLS - 1), NA_COLS - 1) + (NA_COLS - 1)
    h = rpb.shape[0]
    by_col = jnp.take(rpb.astype(F32), jnp.asarray(dc.reshape(-1)), axis=2)
    by_row = jnp.take(by_col, jnp.asarray(dr.reshape(-1)), axis=1)
    bias = by_row.reshape(h, 3, qr_n, 3 * qr_n, GRID_COLS, GRID_COLS)
    ok = row_ok[None, :, :, :, None, None] & col_ok[None, None, None, None, :, :]
    bias = jnp.where(ok, bias * LOG2E, NEG_INF)
    bias = jnp.transpose(bias, (1, 0, 2, 4, 3, 5))
    return bias.reshape(3, h, qr_n * GRID_COLS, 3 * qr_n * GRID_COLS)


def _nattn(qkv, bias_tab, segs, *, nheads):
    nt = qkv.shape[0]
    tq = NBR_QROWS * GRID_COLS
    nh = NBR_HEADS_PER_STEP
    hw = nh * HEAD
    ng = nheads // nh
    blocks = tuple((rows // tq, t // tq) for rows, t in segs)
    assert all(per_seq >= 3 for _, per_seq in blocks)

    def kv_spec(g0, d):
        return pl.BlockSpec((tq, hw), lambda g, b: (_neighbour_block(b, d, blocks), g0 + g))

    def bias_idx(g, b):
        j, nj = _seq_block(b, blocks)
        return (jnp.where(j == 0, 0, jnp.where(j == nj - 1, 2, 1)), g, 0, 0)

    kern = functools.partial(_nattn_kernel, c2=(HEAD ** -0.5) * LOG2E, nh=nh)
    return pl.pallas_call(
        kern,
        grid=(ng, nt // tq),
        in_specs=[pl.BlockSpec((tq, hw), lambda g, b: (b, g)),
                  kv_spec(ng, -1), kv_spec(ng, 0), kv_spec(ng, 1),
                  kv_spec(2 * ng, -1), kv_spec(2 * ng, 0), kv_spec(2 * ng, 1),
                  pl.BlockSpec((None, nh, tq, 3 * tq), bias_idx)],
        out_specs=pl.BlockSpec((tq, hw), lambda g, b: (b, g)),
        out_shape=jax.ShapeDtypeStruct((nt, nheads * HEAD), BF16),
        compiler_params=_cparams(("parallel", "arbitrary")),
        name="nbr_attn",
    )(*([qkv] * 7 + [bias_tab]))


def _outproj_kernel(*refs):
    *aw, x_ref, o_ref = refs
    n = len(aw) // 2
    y = x_ref[...]
    for a_ref, w_ref in zip(aw[:n], aw[n:]):
        y = y + jnp.dot(a_ref[...], w_ref[...], preferred_element_type=F32)
    o_ref[...] = y


def _outproj(parts, weights, x, *, tm):
    nt, d = x.shape
    in_specs = ([pl.BlockSpec((tm, a.shape[1]), lambda i: (i, 0)) for a in parts]
                + [pl.BlockSpec(w.shape, lambda i: (0, 0)) for w in weights]
                + [pl.BlockSpec((tm, d), lambda i: (i, 0))])
    return pl.pallas_call(
        _outproj_kernel,
        grid=(nt // tm,),
        in_specs=in_specs,
        out_specs=pl.BlockSpec((tm, d), lambda i: (i, 0)),
        out_shape=jax.ShapeDtypeStruct((nt, d), F32),
        compiler_params=_cparams(("parallel",)),
        name="out_proj",
    )(*parts, *weights, x)


def _router_kernel(x_ref, g_ref, wr_ref, h_ref, aff_ref):
    h = _rms(x_ref[...], g_ref[...])
    h_ref[...] = h
    logits = lax.dot_general(wr_ref[...], h.astype(BF16), (((1,), (1,)), ((), ())),
                             preferred_element_type=F32)
    m = jnp.max(logits, axis=0, keepdims=True)
    p = jnp.exp(logits - m)
    aff_ref[...] = p / jnp.sum(p, axis=0, keepdims=True)


def _router(x, g, wr_t, *, tm):
    nt, d = x.shape
    e = wr_t.shape[0]
    return pl.pallas_call(
        _router_kernel,
        grid=(nt // tm,),
        in_specs=[pl.BlockSpec((tm, d), lambda i: (i, 0)),
                  pl.BlockSpec((1, d), lambda i: (0, 0)),
                  pl.BlockSpec((e, d), lambda i: (0, 0))],
        out_specs=[pl.BlockSpec((tm, d), lambda i: (i, 0)),
                   pl.BlockSpec((e, tm), lambda i: (0, i))],
        out_shape=[jax.ShapeDtypeStruct((nt, d), F32), jax.ShapeDtypeStruct((e, nt), F32)],
        compiler_params=_cparams(("parallel",)),
        name="router",
    )(x, g.reshape(1, d), wr_t)


def _split3(a):
    hi = a.astype(BF16)
    r1 = a - hi.astype(F32)
    mid = r1.astype(BF16)
    lo = (r1 - mid.astype(F32)).astype(BF16)
    return hi, mid, lo


def _select_kernel(a_ref, idx_ref, gate_ref, *, cap, nrows):
    a = a_ref[0]
    shape = a.shape
    ones_col = jnp.ones((nrows, ROW_TOKENS), BF16)

    def ind(mask):
        return jnp.where(mask, 1.0, 0.0)

    def count(mask):
        return jnp.sum(jnp.sum(ind(mask), axis=0, keepdims=True), axis=1, keepdims=True)

    def step(_, st):
        lo, hi, found, tf = st
        mid = lo + ((hi - lo) >> 1)
        thr = lax.bitcast_convert_type(mid, F32)
        c = count(a >= thr)
        ge = c >= cap
        hit = (c == cap) & (found == 0)
        return (jnp.where(ge, mid, lo), jnp.where(ge, hi, mid),
                jnp.where(hit, 1, found), jnp.where(hit, mid, tf))

    z = jnp.zeros((1, 1), I32)
    lo, hi, found, tf = lax.fori_loop(0, 31, step, (z, z + 0x7F800000, z, z))
    v = lax.bitcast_convert_type(lo, F32)
    gt = a > v
    eq = a == v
    need = cap - count(gt)
    sub = lax.broadcasted_iota(I32, (ROW_TOKENS, ROW_TOKENS), 0)
    lane = lax.broadcasted_iota(I32, (ROW_TOKENS, ROW_TOKENS), 1)
    l_incl = ind(lane <= sub).astype(BF16)
    rr = lax.broadcasted_iota(I32, (nrows, nrows), 0)
    rc = lax.broadcasted_iota(I32, (nrows, nrows), 1)
    u_strict = ind(rr < rc).astype(BF16)

    def prefix(mask_bf):
        cin = jnp.dot(l_incl, mask_bf, preferred_element_type=F32)
        tot = jnp.broadcast_to(cin[ROW_TOKENS - 1:ROW_TOKENS, :], (8, nrows))
        cex = jnp.dot(tot.astype(BF16), u_strict, preferred_element_type=F32)[0:1, :]
        return cin, cin[ROW_TOKENS - 1:ROW_TOKENS, :], cex

    eq_f = ind(eq)
    ceq, _, ceq_off = prefix(eq_f.astype(BF16))
    eq_rank = ceq - eq_f + ceq_off
    sel_tie = ind(gt | (eq & (eq_rank < need)))
    sel = jnp.where(found > 0, ind(a >= lax.bitcast_convert_type(tf, F32)), sel_tie)

    cin, tot, cex = prefix(sel.astype(BF16))
    cinc = cex + tot
    pos = lax.broadcasted_iota(I32, (cap, nrows), 0).astype(F32)
    before = cinc <= pos
    before_bf = jnp.where(before, 1.0, 0.0).astype(BF16)
    rstar = jnp.dot(before_bf, ones_col, preferred_element_type=F32)
    skipped = jnp.dot(jnp.where(before, tot, 0.0).astype(BF16), ones_col, preferred_element_type=F32)
    pe = lax.broadcasted_iota(I32, (cap, ROW_TOKENS), 0).astype(F32) - skipped
    rid = lax.broadcasted_iota(I32, (cap, nrows), 1).astype(F32)
    rowsel = jnp.where(rid == rstar[:, 0:1], 1.0, 0.0).astype(BF16)
    nt_dims = (((1,), (1,)), ((), ()))
    cin_g = lax.dot_general(rowsel, cin.astype(BF16), nt_dims, preferred_element_type=F32)
    lstar = jnp.dot(jnp.where(cin_g <= pe, 1.0, 0.0).astype(BF16),
                    jnp.ones((ROW_TOKENS, ROW_TOKENS), BF16), preferred_element_type=F32)
    idx_ref[0] = (rstar * ROW_TOKENS + lstar).astype(I32)
    a_g = sum(lax.dot_general(rowsel, part, nt_dims, preferred_element_type=F32) for part in _split3(a))
    lid = lax.broadcasted_iota(I32, (cap, ROW_TOKENS), 1).astype(F32)
    gate = jnp.sum(jnp.where(lid == lstar, a_g, 0.0), axis=1, keepdims=True)
    gate_ref[0] = jnp.broadcast_to(gate, (cap, ROW_TOKENS))


def _select(aff_t, *, cap):
    e, n = aff_t.shape
    r = n // ROW_TOKENS
    nrows = max(r, ROW_TOKENS)
    a2 = jnp.swapaxes(aff_t.reshape(e, r, ROW_TOKENS), 1, 2)
    if nrows > r:
        a2 = jnp.pad(a2, ((0, 0), (0, 0), (0, nrows - r)), constant_values=-1.0)
    kern = functools.partial(_select_kernel, cap=cap, nrows=nrows)
    return pl.pallas_call(
        kern,
        grid=(e,),
        in_specs=[pl.BlockSpec((1, ROW_TOKENS, nrows), lambda i: (i, 0, 0))],
        out_specs=[pl.BlockSpec((1, cap, ROW_TOKENS), lambda i: (i, 0, 0)),
                   pl.BlockSpec((1, cap, ROW_TOKENS), lambda i: (i, 0, 0))],
        out_shape=[jax.ShapeDtypeStruct((e, cap, ROW_TOKENS), I32),
                   jax.ShapeDtypeStruct((e, cap, ROW_TOKENS), F32)],
        compiler_params=_cparams(("parallel",)),
        name="expert_select",
    )(a2)


def _row_copy(src_hbm, dst_vmem, sem, token, slot):
    return pltpu.make_async_copy(src_hbm.at[pl.ds(token, 1)], dst_vmem.at[pl.ds(slot, 1)], sem)


def _gather_kernel(idx_ref, h_hbm, o_ref, buf, sem, *, tg):
    base = pl.program_id(0) * tg

    copies = [_row_copy(h_hbm, buf, sem, idx_ref[base + s], s) for s in range(tg)]
    for cp in copies:
        cp.start()
    for cp in copies:
        cp.wait()
    o_ref[...] = buf[...].astype(o_ref.dtype)


def _gather(h, idx_flat, *, tg):
    nt, d = h.shape
    n = idx_flat.shape[0]
    kern = functools.partial(_gather_kernel, tg=tg)
    return pl.pallas_call(
        kern,
        grid_spec=pltpu.PrefetchScalarGridSpec(
            num_scalar_prefetch=1,
            grid=(n // tg,),
            in_specs=[pl.BlockSpec(memory_space=pl.ANY)],
            out_specs=pl.BlockSpec((tg, d), lambda i, idx: (i, 0)),
            scratch_shapes=[pltpu.VMEM((tg, d), F32), pltpu.SemaphoreType.DMA(())],
        ),
        out_shape=jax.ShapeDtypeStruct((n, d), BF16),
        compiler_params=_cparams(("arbitrary",)),
        name="token_gather",
    )(idx_flat, h)


def _ffn_kernel(x_ref, gate_ref, wg_ref, wu_ref, wd_ref, o_ref, *, nf):
    f = pl.program_id(2)

    @pl.when(f == 0)
    def _():
        o_ref[0] = jnp.zeros(o_ref.shape[1:], F32)

    x = x_ref[0]
    g = jnp.dot(x, wg_ref[0, 0].astype(BF16), preferred_element_type=F32)
    u = jnp.dot(x, wu_ref[0, 0].astype(BF16), preferred_element_type=F32)
    a = (g * jax.nn.sigmoid(g)) * u
    o_ref[0] += jnp.dot(a.astype(BF16), wd_ref[0, 0].astype(BF16), preferred_element_type=F32)

    @pl.when(f == nf - 1)
    def _():
        o_ref[0] = o_ref[0] * gate_ref[0][:, 0:1]


def _ffn(xg, gate, w_gate, w_up, w_down, layer, *, tm, tf):
    e, capt, d = xg.shape
    dff = w_gate.shape[-1]
    nf = dff // tf
    kern = functools.partial(_ffn_kernel, nf=nf)
    return pl.pallas_call(
        kern,
        grid=(e, capt // tm, nf),
        in_specs=[pl.BlockSpec((1, tm, d), lambda ei, i, f: (ei, i, 0)),
                  pl.BlockSpec((1, tm, ROW_TOKENS), lambda ei, i, f: (ei, i, 0)),
                  pl.BlockSpec((1, 1, d, tf), lambda ei, i, f: (layer, ei, 0, f)),
                  pl.BlockSpec((1, 1, d, tf), lambda ei, i, f: (layer, ei, 0, f)),
                  pl.BlockSpec((1, 1, tf, d), lambda ei, i, f: (layer, ei, f, 0))],
        out_specs=pl.BlockSpec((1, tm, d), lambda ei, i, f: (ei, i, 0)),
        out_shape=jax.ShapeDtypeStruct((e, capt, d), F32),
        compiler_params=_cparams(("parallel", "parallel", "arbitrary")),
        name="expert_ffn",
    )(xg, gate, w_gate, w_up, w_down)


def _combine_kernel(idx_ref, y_ref, x_hbm, o_hbm, buf, sem_in, sem_out, *, tg):
    del x_hbm
    base = pl.program_id(0) * tg

    tokens = [idx_ref[base + s] for s in range(tg)]
    reads = [_row_copy(o_hbm, buf, sem_in, t, s) for s, t in enumerate(tokens)]
    writes = [pltpu.make_async_copy(buf.at[pl.ds(s, 1)], o_hbm.at[pl.ds(t, 1)], sem_out)
              for s, t in enumerate(tokens)]
    for cp in reads:
        cp.start()
    for cp in reads:
        cp.wait()
    buf[...] = buf[...] + y_ref[...]
    for cp in writes:
        cp.start()
    for cp in writes:
        cp.wait()


def _combine(x, yeg, idx_flat, *, tg):
    nt, d = x.shape
    n = idx_flat.shape[0]
    kern = functools.partial(_combine_kernel, tg=tg)
    return pl.pallas_call(
        kern,
        grid_spec=pltpu.PrefetchScalarGridSpec(
            num_scalar_prefetch=1,
            grid=(n // tg,),
            in_specs=[pl.BlockSpec((tg, d), lambda i, idx: (i, 0)),
                      pl.BlockSpec(memory_space=pl.ANY)],
            out_specs=pl.BlockSpec(memory_space=pl.ANY),
            scratch_shapes=[pltpu.VMEM((tg, d), F32), pltpu.SemaphoreType.DMA(()),
                            pltpu.SemaphoreType.DMA(())],
        ),
        out_shape=jax.ShapeDtypeStruct((nt, d), F32),
        input_output_aliases={2: 0},
        compiler_params=_cparams(("arbitrary",)),
        name="expert_combine",
    )(idx_flat, yeg, x)


def _moe(x, segs, g, w_router_l, w_gate, w_up, w_down, layer):
    nt, d = x.shape
    h, aff_t = _router(x, g, jnp.swapaxes(w_router_l, 0, 1).astype(BF16), tm=512)
    ids, gates = [], []
    row = 0
    for rows, _ in segs:
        cap = 2 * rows // N_EXP
        idx, gate = _select(aff_t[:, row:row + rows], cap=cap)
        ids.append(idx[:, :, 0] + row)
        gates.append(gate)
        row += rows
    idx_all = jnp.concatenate(ids, axis=1)
    gate_all = jnp.concatenate(gates, axis=1)
    capt = idx_all.shape[1]
    idx_flat = idx_all.reshape(-1)
    tg = 256
    xg = _gather(h, idx_flat, tg=tg).reshape(N_EXP, capt, d)
    tm = 1024 if capt % 1024 == 0 else capt
    yeg = _ffn(xg, gate_all, w_gate, w_up, w_down, layer, tm=tm, tf=256 if w_gate.shape[-1] % 256 == 0 else w_gate.shape[-1])
    return _combine(x, yeg.reshape(N_EXP * capt, d), idx_flat, tg=tg)


def _trunk(x, segs, attn_norm, ffn_norm, w_in_ab, w_out_ab, q_norm_a, k_norm_a, q_norm_b, k_norm_b, sink_b,
           w_in_c, w_out_c, q_norm_c, k_norm_c, rpb_c, w_router, w_gate, w_up, w_down):
    depth = attn_norm.shape[0]
    ones = jnp.ones((HEAD,), F32)
    tm_proj = 1024
    for layer in range(depth):
        if layer % 2 == 0:
            e = layer // 2
            gain = jnp.concatenate([jnp.tile(q_norm_a[e], 8), jnp.tile(k_norm_a[e], 2), jnp.tile(ones, 2),
                                    jnp.tile(q_norm_b[e], 8), jnp.tile(k_norm_b[e], 2), jnp.tile(ones, 2)])
            qkv = _proj(x, attn_norm[layer], w_in_ab[e].astype(BF16), gain, segs,
                        tile_kinds=(((0, 2), "rrrr"), ((2, 3), "rrpp"), ((3, 5), "nnnn"), ((5, 6), "nnpp")),
                        tm=tm_proj)
            oa = []
            row = 0
            for rows, t in segs:
                oa.append(_gattn(qkv, row_off=row, nseq=rows // t, t=t, tq=min(GATTN_TQ, t),
                                 tk=min(GATTN_TK, t // 2)))
                row += rows
            oa = jnp.concatenate(oa, axis=0)
            slopes = 2.0 ** (-8.0 * jnp.arange(1, 9, dtype=F32) / 8)
            ob = _wattn(qkv, slopes, sink_b[e].astype(F32), segs)
            w_out = w_out_ab[e].astype(BF16)
            ka = oa.shape[1]
            x = _outproj([oa, ob], [w_out[:ka], w_out[ka:]], x, tm=512)
        else:
            c = layer // 2
            gain = jnp.concatenate([jnp.tile(q_norm_c[c], 16), jnp.tile(k_norm_c[c], 16), jnp.tile(ones, 16)])
            qkv = _proj(x, attn_norm[layer], w_in_c[c].astype(BF16), gain, segs,
                        tile_kinds=(((0, 8), "nnnn"), ((8, 12), "pppp")), tm=tm_proj)
            tab = _nbr_bias_table(rpb_c[c])
            o = _nattn(qkv, tab, segs, nheads=16)
            x = _outproj([o], [w_out_c[c].astype(BF16)], x, tm=512)
        x = _moe(x, segs, ffn_norm[layer], w_router[layer], w_gate, w_up, w_down, layer)
    return x


def kernel(x_prompt, x_sample, attn_norm, ffn_norm, w_in_ab, w_out_ab, q_norm_a, k_norm_a, q_norm_b, k_norm_b,
           sink_b, w_in_c, w_out_c, q_norm_c, k_norm_c, rpb_c, w_router, w_gate, w_up, w_down):
    bp, tp, d = x_prompt.shape
    bs, ts, _ = x_sample.shape
    x = jnp.concatenate([x_sample.reshape(bs * ts, d), x_prompt.reshape(bp * tp, d)], axis=0)
    segs = ((bs * ts, ts), (bp * tp, tp))
    y = _trunk(x, segs, attn_norm, ffn_norm, w_in_ab, w_out_ab, q_norm_a, k_norm_a, q_norm_b, k_norm_b,
               sink_b, w_in_c, w_out_c, q_norm_c, k_norm_c, rpb_c, w_router, w_gate, w_up, w_down)
    y_sample = y[:bs * ts].reshape(bs, ts, d)
    y_prompt = y[bs * ts:].reshape(bp, tp, d)
    return (y_prompt, y_sample)
```

```python
import functools

import jax
import jax.numpy as jnp
import numpy as np
from jax import lax
from jax.experimental import pallas as pl
from jax.experimental.pallas import tpu as pltpu

F32 = jnp.float32
BF16 = jnp.bfloat16
I32 = jnp.int32

HEAD = 128
GRID_COLS = 64
QBLK = 128
N_EXP = 16
NORM_EPS = 1e-6
NEG_INF = -1e30
ROPE_THETA = 10000.0
NA_ROWS, NA_COLS = 8, 16
NBR_QROWS = 4
NBR_HEADS_PER_STEP = 4
LOG2E = float(np.log2(np.e))
GATTN_TQ, GATTN_TK = 128, 2048
ROW_TOKENS = 128
VMEM_LIMIT = 56 * 1024 * 1024


def _cparams(sem):
    return pltpu.CompilerParams(dimension_semantics=sem, vmem_limit_bytes=VMEM_LIMIT)


def _rms(x, g):
    r = lax.rsqrt(jnp.mean(x * x, axis=-1, keepdims=True) + NORM_EPS)
    return (x * r) * g


def _swap_halves(y):
    lane = lax.broadcasted_iota(I32, y.shape, 1)
    return jnp.where((lane % 64) < 32, pltpu.roll(y, 96, 1), pltpu.roll(y, 32, 1))


def _proj_kernel(*refs, tile_kinds, has_rope):
    if has_rope:
        x_ref, g_ref, w_ref, gain_ref, cos_ref, sin_ref, o_ref, xn_ref = refs
    else:
        x_ref, g_ref, w_ref, gain_ref, o_ref, xn_ref = refs
    j = pl.program_id(1)

    @pl.when(j == 0)
    def _():
        xn_ref[...] = _rms(x_ref[...], g_ref[...]).astype(BF16)

    def tile(kinds):
        acc = jnp.dot(xn_ref[...], w_ref[...], preferred_element_type=F32)
        for h, kind in enumerate(kinds):
            cols = slice(h * HEAD, (h + 1) * HEAD)
            y = acc[:, cols]
            if kind in "rn":
                y = _rms(y, gain_ref[:, cols])
            if kind == "r":
                y = y * cos_ref[...] + _swap_halves(y) * sin_ref[...]
            o_ref[:, cols] = y.astype(o_ref.dtype)

    for (lo, hi), kinds in tile_kinds:
        pl.when((j >= lo) & (j < hi))(functools.partial(tile, kinds))


def _rope_tables(t_max):
    quarter = HEAD // 4
    t = jnp.arange(t_max)
    inv = ROPE_THETA ** (-jnp.arange(quarter, dtype=F32) / quarter)
    ang_r = (t // GRID_COLS).astype(F32)[:, None] * inv
    ang_c = (t % GRID_COLS).astype(F32)[:, None] * inv
    cr, sr, cc, sc = jnp.cos(ang_r), jnp.sin(ang_r), jnp.cos(ang_c), jnp.sin(ang_c)
    return (jnp.concatenate([cr, cr, cc, cc], axis=-1),
            jnp.concatenate([-sr, sr, -sc, sc], axis=-1))


def _proj(x, g, w, gain, segs, *, tile_kinds, tm):
    nt, d = x.shape
    nout = w.shape[1]
    tn = HEAD * len(tile_kinds[0][1])
    assert tile_kinds[-1][0][1] * tn == nout
    rope_tiles = any("r" in kinds for _, kinds in tile_kinds)
    (rows0, t0), (rows1, t1) = segs
    nb0 = rows0 // tm

    def pos_block(i):
        return jnp.where(i < nb0, i % (t0 // tm), (i - nb0) % (t1 // tm))

    in_specs = [
        pl.BlockSpec((tm, d), lambda i, j: (i, 0)),
        pl.BlockSpec((1, d), lambda i, j: (0, 0)),
        pl.BlockSpec((d, tn), lambda i, j: (0, j)),
        pl.BlockSpec((1, tn), lambda i, j: (0, j)),
    ]
    args = [x, g.reshape(1, d), w, gain.reshape(1, nout)]
    if rope_tiles:
        cos, sin = _rope_tables(max(t0, t1))
        in_specs += [pl.BlockSpec((tm, HEAD), lambda i, j: (pos_block(i), 0))] * 2
        args += [cos, sin]
    kern = functools.partial(_proj_kernel, tile_kinds=tile_kinds, has_rope=rope_tiles)
    return pl.pallas_call(
        kern,
        grid=(nt // tm, nout // tn),
        in_specs=in_specs,
        out_specs=pl.BlockSpec((tm, tn), lambda i, j: (i, j)),
        out_shape=jax.ShapeDtypeStruct((nt, nout), BF16),
        scratch_shapes=[pltpu.VMEM((tm, d), BF16)],
        compiler_params=_cparams(("parallel", "arbitrary")),
        name="in_proj",
    )(*args)


def _stack_heads(q, n):
    return jnp.concatenate([q[:, g * HEAD:(g + 1) * HEAD] for g in range(n)], axis=0)


def _gattn_kernel(q_ref, k_ref, v_ref, o_ref, s0_ref, s1_ref, m_ref, l_ref, acc_ref, *, tq, tk, nkc, group):
    c2 = (HEAD ** -0.5) * float(np.log2(np.e))
    q4 = _stack_heads(q_ref[...], group)
    m_ref[...] = jnp.full(m_ref.shape, -jnp.inf, F32)
    l_ref[...] = jnp.zeros(l_ref.shape, F32)
    acc_ref[...] = jnp.zeros(acc_ref.shape, F32)
    nlane = tk // HEAD

    def qk(c, dst_ref):
        kc = k_ref[pl.ds(pl.multiple_of(c * tk, tk), tk), :]
        dst_ref[...] = lax.dot_general(q4, kc, (((1,), (1,)), ((), ())), preferred_element_type=F32)

    def softmax_pv(c, src_ref):
        s = src_ref[...]
        lane_max = functools.reduce(jnp.maximum, [s[:, j * HEAD:(j + 1) * HEAD] for j in range(nlane)])
        m_prev = m_ref[...]
        m_new = jnp.maximum(m_prev, jnp.max(lane_max, axis=-1, keepdims=True) * c2)
        alpha = jnp.exp2(m_prev - m_new)
        p = jnp.exp2(s * c2 - m_new)
        lane_sum = functools.reduce(jnp.add, [p[:, j * HEAD:(j + 1) * HEAD] for j in range(nlane)])
        l_ref[...] = alpha * l_ref[...] + lane_sum
        vc = v_ref[pl.ds(pl.multiple_of(c * tk, tk), tk), :]
        acc_ref[...] = alpha * acc_ref[...] + jnp.dot(p.astype(BF16), vc, preferred_element_type=F32)
        m_ref[...] = m_new

    qk(0, s0_ref)

    def pair(c, carry):
        qk(2 * c + 1, s1_ref)
        softmax_pv(2 * c, s0_ref)
        qk(2 * c + 2, s0_ref)
        softmax_pv(2 * c + 1, s1_ref)
        return carry

    lax.fori_loop(0, nkc // 2 - 1, pair, 0)
    qk(nkc - 1, s1_ref)
    softmax_pv(nkc - 2, s0_ref)
    softmax_pv(nkc - 1, s1_ref)
    out = acc_ref[...] * (1.0 / jnp.sum(l_ref[...], axis=-1, keepdims=True))
    for g in range(group):
        o_ref[:, g * HEAD:(g + 1) * HEAD] = out[g * tq:(g + 1) * tq].astype(o_ref.dtype)


def _gattn(qkv, *, row_off, nseq, t, tq, tk):
    group, kvh = 4, 2
    nq = t // tq
    nkc = t // tk
    assert nkc % 2 == 0
    qoff = row_off // tq
    soff = row_off // t
    qw = group * HEAD
    kern = functools.partial(_gattn_kernel, tq=tq, tk=tk, nkc=nkc, group=group)
    return pl.pallas_call(
        kern,
        grid=(nseq, kvh, nq),
        in_specs=[pl.BlockSpec((tq, qw), lambda b, h, i: (qoff + b * nq + i, h)),
                  pl.BlockSpec((t, HEAD), lambda b, h, i: (soff + b, 8 + h)),
                  pl.BlockSpec((t, HEAD), lambda b, h, i: (soff + b, 10 + h))],
        out_specs=pl.BlockSpec((tq, qw), lambda b, h, i: (b * nq + i, h)),
        out_shape=jax.ShapeDtypeStruct((nseq * t, kvh * qw), BF16),
        scratch_shapes=[pltpu.VMEM((group * tq, tk), F32), pltpu.VMEM((group * tq, tk), F32),
                        pltpu.VMEM((group * tq, 1), F32), pltpu.VMEM((group * tq, HEAD), F32),
                        pltpu.VMEM((group * tq, HEAD), F32)],
        compiler_params=_cparams(("parallel", "parallel", "arbitrary")),
        name="global_attn",
    )(qkv, qkv, qkv)


def _seq_block(b, blocks):
    (n0, s0), (_, s1) = blocks
    first = b < n0
    i = jnp.where(first, b % s0, (b - n0) % s1)
    return i, jnp.where(first, s0, s1)


def _neighbour_block(b, d, blocks):
    i, n = _seq_block(b, blocks)
    return b - i + jnp.clip(i + d, 0, n - 1)


def _wattn_kernel(slope_ref, sink_ref, q_ref, kp_ref, kc_ref, kn_ref, vp_ref, vc_ref, vn_ref, o_ref,
                  *, blocks, scale, group):
    kh = pl.program_id(0)
    i, nb = _seq_block(pl.program_id(1), blocks)
    q4 = _stack_heads(q_ref[...], group)
    kcat = jnp.concatenate([kp_ref[...], kc_ref[...], kn_ref[...]], axis=0)
    vcat = jnp.concatenate([vp_ref[...], vc_ref[...], vn_ref[...]], axis=0)
    rows, span = group * QBLK, 3 * QBLK
    s = lax.dot_general(q4, kcat, (((1,), (1,)), ((), ())), preferred_element_type=F32) * scale
    a = lax.broadcasted_iota(I32, (rows, span), 0) % QBLK
    jj = lax.broadcasted_iota(I32, (rows, span), 1)
    rel = a + QBLK - jj
    dist = jnp.abs(rel)
    ok = (dist <= QBLK) & ((jj >= QBLK) | (i > 0)) & ((jj < 2 * QBLK) | (i < nb - 1))
    rgrp = lax.broadcasted_iota(I32, (rows, 1), 0) // QBLK
    slope = jnp.zeros((rows, 1), F32)
    sink = jnp.zeros((rows, 1), F32)
    for g in range(group):
        slope = jnp.where(rgrp == g, slope_ref[kh * group + g], slope)
        sink = jnp.where(rgrp == g, sink_ref[kh * group + g], sink)
    s = s + (-slope) * dist.astype(F32)
    s = jnp.where(ok, s, NEG_INF)
    m = jnp.maximum(jnp.max(s, axis=-1, keepdims=True), sink)
    p = jnp.exp(s - m)
    denom = jnp.sum(p, axis=-1, keepdims=True) + jnp.exp(sink - m)
    p = p * (1.0 / denom)
    out = jnp.dot(p.astype(BF16), vcat, preferred_element_type=F32)
    for g in range(group):
        o_ref[:, g * HEAD:(g + 1) * HEAD] = out[g * QBLK:(g + 1) * QBLK].astype(o_ref.dtype)


def _wattn(qkv, slopes, sink, segs):
    nt = qkv.shape[0]
    group, kvh = 4, 2
    qw = group * HEAD
    blocks = tuple((rows // QBLK, t // QBLK) for rows, t in segs)

    def kv_spec(col0, d):
        return pl.BlockSpec((QBLK, HEAD), lambda h, b: (_neighbour_block(b, d, blocks), col0 + h))

    smem = pl.BlockSpec(memory_space=pltpu.SMEM)
    kern = functools.partial(_wattn_kernel, blocks=blocks, scale=HEAD ** -0.5, group=group)
    return pl.pallas_call(
        kern,
        grid=(kvh, nt // QBLK),
        in_specs=[smem, smem,
                  pl.BlockSpec((QBLK, qw), lambda h, b: (b, 3 + h)),
                  kv_spec(20, -1), kv_spec(20, 0), kv_spec(20, 1),
                  kv_spec(22, -1), kv_spec(22, 0), kv_spec(22, 1)],
        out_specs=pl.BlockSpec((QBLK, qw), lambda h, b: (b, h)),
        out_shape=jax.ShapeDtypeStruct((nt, kvh * qw), BF16),
        compiler_params=_cparams(("parallel", "arbitrary")),
        name="window_attn",
    )(slopes, sink, qkv, qkv, qkv, qkv, qkv, qkv, qkv)


def _nattn_kernel(q_ref, kp_ref, kc_ref, kn_ref, vp_ref, vc_ref, vn_ref, b_ref, o_ref, *, c2, nh):
    for h in range(nh):
        cols = slice(h * HEAD, (h + 1) * HEAD)
        kcat = jnp.concatenate([kp_ref[:, cols], kc_ref[:, cols], kn_ref[:, cols]], axis=0)
        vcat = jnp.concatenate([vp_ref[:, cols], vc_ref[:, cols], vn_ref[:, cols]], axis=0)
        s = lax.dot_general(q_ref[:, cols], kcat, (((1,), (1,)), ((), ())), preferred_element_type=F32)
        s = s * c2 + b_ref[h]
        nlane = s.shape[1] // HEAD
        lane_max = functools.reduce(jnp.maximum, [s[:, j * HEAD:(j + 1) * HEAD] for j in range(nlane)])
        p = jnp.exp2(s - jnp.max(lane_max, axis=-1, keepdims=True))
        lane_sum = functools.reduce(jnp.add, [p[:, j * HEAD:(j + 1) * HEAD] for j in range(nlane)])
        l = jnp.sum(lane_sum, axis=-1, keepdims=True)
        out = jnp.dot(p.astype(BF16), vcat, preferred_element_type=F32) * (1.0 / l)
        o_ref[:, cols] = out.astype(o_ref.dtype)


def _nbr_bias_table(rpb):
    qr_n, rows = NBR_QROWS, 3 * NBR_QROWS
    blk = np.arange(3)[:, None, None]
    r = blk * qr_n + np.arange(qr_n)[None, :, None]
    kr = (blk - 1) * qr_n + np.arange(3 * qr_n)[None, None, :]
    rs = np.clip(r - NA_ROWS // 2, 0, rows - NA_ROWS)
    row_ok = (kr >= rs) & (kr < rs + NA_ROWS)
    dr = np.clip(kr - r + (NA_ROWS - 1), 0, 2 * NA_ROWS - 2)
    qc = np.arange(GRID_COLS)[:, None]
    kc = np.arange(GRID_COLS)[None, :]
    cs = np.clip(qc - NA_COLS // 2, 0, GRID_COLS - NA_COLS)
    col_ok = (kc >= cs) & (kc < cs + NA_COLS)
    dc = np.clip(kc - qc, -(NA_COLS - 1), NA_COLS - 1) + (NA_COLS - 1)
    h = rpb.shape[0]
    by_col = jnp.take(rpb.astype(F32), jnp.asarray(dc.reshape(-1)), axis=2)
    by_row = jnp.take(by_col, jnp.asarray(dr.reshape(-1)), axis=1)
    bias = by_row.reshape(h, 3, qr_n, 3 * qr_n, GRID_COLS, GRID_COLS)
    ok = row_ok[None, :, :, :, None, None] & col_ok[None, None, None, None, :, :]
    bias = jnp.where(ok, bias * LOG2E, NEG_INF)
    bias = jnp.transpose(bias, (1, 0, 2, 4, 3, 5))
    return bias.reshape(3, h, qr_n * GRID_COLS, 3 * qr_n * GRID_COLS)


def _nattn(qkv, bias_tab, segs, *, nheads):
    nt = qkv.shape[0]
    tq = NBR_QROWS * GRID_COLS
    nh = NBR_HEADS_PER_STEP
    hw = nh * HEAD
    ng = nheads // nh
    blocks = tuple((rows // tq, t // tq) for rows, t in segs)
    assert all(per_seq >= 3 for _, per_seq in blocks)

    def kv_spec(g0, d):
        return pl.BlockSpec((tq, hw), lambda g, b: (_neighbour_block(b, d, blocks), g0 + g))

    def bias_idx(g, b):
        j, nj = _seq_block(b, blocks)
        return (jnp.where(j == 0, 0, jnp.where(j == nj - 1, 2, 1)), g, 0, 0)

    kern = functools.partial(_nattn_kernel, c2=(HEAD ** -0.5) * LOG2E, nh=nh)
    return pl.pallas_call(
        kern,
        grid=(ng, nt // tq),
        in_specs=[pl.BlockSpec((tq, hw), lambda g, b: (b, g)),
                  kv_spec(ng, -1), kv_spec(ng, 0), kv_spec(ng, 1),
                  kv_spec(2 * ng, -1), kv_spec(2 * ng, 0), kv_spec(2 * ng, 1),
                  pl.BlockSpec((None, nh, tq, 3 * tq), bias_idx)],
        out_specs=pl.BlockSpec((tq, hw), lambda g, b: (b, g)),
        out_shape=jax.ShapeDtypeStruct((nt, nheads * HEAD), BF16),
        compiler_params=_cparams(("parallel", "arbitrary")),
        name="nbr_attn",
    )(*([qkv] * 7 + [bias_tab]))


def _outproj_kernel(*refs):
    *aw, x_ref, o_ref = refs
    n = len(aw) // 2
    y = x_ref[...]
    for a_ref, w_ref in zip(aw[:n], aw[n:]):
        y = y + jnp.dot(a_ref[...], w_ref[...], preferred_element_type=F32)
    o_ref[...] = y


def _outproj(parts, weights, x, *, tm):
    nt, d = x.shape
    in_specs = ([pl.BlockSpec((tm, a.shape[1]), lambda i: (i, 0)) for a in parts]
                + [pl.BlockSpec(w.shape, lambda i: (0, 0)) for w in weights]
                + [pl.BlockSpec((tm, d), lambda i: (i, 0))])
    return pl.pallas_call(
        _outproj_kernel,
        grid=(nt // tm,),
        in_specs=in_specs,
        out_specs=pl.BlockSpec((tm, d), lambda i: (i, 0)),
        out_shape=jax.ShapeDtypeStruct((nt, d), F32),
        compiler_params=_cparams(("parallel",)),
        name="out_proj",
    )(*parts, *weights, x)


def _router_kernel(x_ref, g_ref, wr_ref, h_ref, aff_ref):
    h = _rms(x_ref[...], g_ref[...])
    h_ref[...] = h
    logits = lax.dot_general(wr_ref[...], h.astype(BF16), (((1,), (1,)), ((), ())),
                             preferred_element_type=F32)
    m = jnp.max(logits, axis=0, keepdims=True)
    p = jnp.exp(logits - m)
    aff_ref[...] = p / jnp.sum(p, axis=0, keepdims=True)


def _router(x, g, wr_t, *, tm):
    nt, d = x.shape
    e = wr_t.shape[0]
    return pl.pallas_call(
        _router_kernel,
        grid=(nt // tm,),
        in_specs=[pl.BlockSpec((tm, d), lambda i: (i, 0)),
                  pl.BlockSpec((1, d), lambda i: (0, 0)),
                  pl.BlockSpec((e, d), lambda i: (0, 0))],
        out_specs=[pl.BlockSpec((tm, d), lambda i: (i, 0)),
                   pl.BlockSpec((e, tm), lambda i: (0, i))],
        out_shape=[jax.ShapeDtypeStruct((nt, d), F32), jax.ShapeDtypeStruct((e, nt), F32)],
        compiler_params=_cparams(("parallel",)),
        name="router",
    )(x, g.reshape(1, d), wr_t)


def _split3(a):
    hi = a.astype(BF16)
    r1 = a - hi.astype(F32)
    mid = r1.astype(BF16)
    lo = (r1 - mid.astype(F32)).astype(BF16)
    return hi, mid, lo


def _select_kernel(a_ref, idx_ref, gate_ref, *, cap, nrows):
    a = a_ref[0]
    shape = a.shape
    ones_col = jnp.ones((nrows, ROW_TOKENS), BF16)

    def ind(mask):
        return jnp.where(mask, 1.0, 0.0)

    def count(mask):
        return jnp.sum(jnp.sum(ind(mask), axis=0, keepdims=True), axis=1, keepdims=True)

    def step(_, st):
        lo, hi, found, tf = st
        mid = lo + ((hi - lo) >> 1)
        thr = lax.bitcast_convert_type(mid, F32)
        c = count(a >= thr)
        ge = c >= cap
        hit = (c == cap) & (found == 0)
        return (jnp.where(ge, mid, lo), jnp.where(ge, hi, mid),
                jnp.where(hit, 1, found), jnp.where(hit, mid, tf))

    z = jnp.zeros((1, 1), I32)
    lo, hi, found, tf = lax.fori_loop(0, 31, step, (z, z + 0x7F800000, z, z))
    v = lax.bitcast_convert_type(lo, F32)
    gt = a > v
    eq = a == v
    need = cap - count(gt)
    sub = lax.broadcasted_iota(I32, (ROW_TOKENS, ROW_TOKENS), 0)
    lane = lax.broadcasted_iota(I32, (ROW_TOKENS, ROW_TOKENS), 1)
    l_incl = ind(lane <= sub).astype(BF16)
    rr = lax.broadcasted_iota(I32, (nrows, nrows), 0)
    rc = lax.broadcasted_iota(I32, (nrows, nrows), 1)
    u_strict = ind(rr < rc).astype(BF16)

    def prefix(mask_bf):
        cin = jnp.dot(l_incl, mask_bf, preferred_element_type=F32)
        tot = jnp.broadcast_to(cin[ROW_TOKENS - 1:ROW_TOKENS, :], (8, nrows))
        cex = jnp.dot(tot.astype(BF16), u_strict, preferred_element_type=F32)[0:1, :]
        return cin, cin[ROW_TOKENS - 1:ROW_TOKENS, :], cex

    eq_f = ind(eq)
    ceq, _, ceq_off = prefix(eq_f.astype(BF16))
    eq_rank = ceq - eq_f + ceq_off
    sel_tie = ind(gt | (eq & (eq_rank < need)))
    sel = jnp.where(found > 0, ind(a >= lax.bitcast_convert_type(tf, F32)), sel_tie)

    cin, tot, cex = prefix(sel.astype(BF16))
    cinc = cex + tot
    pos = lax.broadcasted_iota(I32, (cap, nrows), 0).astype(F32)
    before = cinc <= pos
    before_bf = jnp.where(before, 1.0, 0.0).astype(BF16)
    rstar = jnp.dot(before_bf, ones_col, preferred_element_type=F32)
    skipped = jnp.dot(jnp.where(before, tot, 0.0).astype(BF16), ones_col, preferred_element_type=F32)
    pe = lax.broadcasted_iota(I32, (cap, ROW_TOKENS), 0).astype(F32) - skipped
    rid = lax.broadcasted_iota(I32, (cap, nrows), 1).astype(F32)
    rowsel = jnp.where(rid == rstar[:, 0:1], 1.0, 0.0).astype(BF16)
    nt_dims = (((1,), (1,)), ((), ()))
    cin_g = lax.dot_general(rowsel, cin.astype(BF16), nt_dims, preferred_element_type=F32)
    lstar = jnp.dot(jnp.where(cin_g <= pe, 1.0, 0.0).astype(BF16),
                    jnp.ones((ROW_TOKENS, ROW_TOKENS), BF16), preferred_element_type=F32)
    idx_ref[0] = (rstar * ROW_TOKENS + lstar).astype(I32)
    a_g = sum(lax.dot_general(rowsel, part, nt_dims, preferred_element_type=F32) for part in _split3(a))
    lid = lax.broadcasted_iota(I32, (cap, ROW_TOKENS), 1).astype(F32)
    gate = jnp.sum(jnp.where(lid == lstar, a_g, 0.0), axis=1, keepdims=True)
    gate_ref[0] = jnp.broadcast_to(gate, (cap, ROW_TOKENS))


def _select(aff_t, *, cap):
    e, n = aff_t.shape
    r = n // ROW_TOKENS
    nrows = max(r, ROW_TOKENS)
    a2 = jnp.swapaxes(aff_t.reshape(e, r, ROW_TOKENS), 1, 2)
    if nrows > r:
        a2 = jnp.pad(a2, ((0, 0), (0, 0), (0, nrows - r)), constant_values=-1.0)
    kern = functools.partial(_select_kernel, cap=cap, nrows=nrows)
    return pl.pallas_call(
        kern,
        grid=(e,),
        in_specs=[pl.BlockSpec((1, ROW_TOKENS, nrows), lambda i: (i, 0, 0))],
        out_specs=[pl.BlockSpec((1, cap, ROW_TOKENS), lambda i: (i, 0, 0)),
                   pl.BlockSpec((1, cap, ROW_TOKENS), lambda i: (i, 0, 0))],
        out_shape=[jax.ShapeDtypeStruct((e, cap, ROW_TOKENS), I32),
                   jax.ShapeDtypeStruct((e, cap, ROW_TOKENS), F32)],
        compiler_params=_cparams(("parallel",)),
        name="expert_select",
    )(a2)


def _row_copy(src_hbm, dst_vmem, sem, token, slot):
    return pltpu.make_async_copy(src_hbm.at[pl.ds(token, 1)], dst_vmem.at[pl.ds(slot, 1)], sem)


def _gather_kernel(idx_ref, h_hbm, o_ref, buf, sem, *, tg, nsteps):
    i = pl.program_id(0)

    def copies(step, half):
        return [_row_copy(h_hbm, buf.at[half], sem.at[half], idx_ref[step * tg + s], s) for s in range(tg)]

    @pl.when(i == 0)
    def _():
        for cp in copies(0, 0):
            cp.start()

    @pl.when(i + 1 < nsteps)
    def _():
        for cp in copies(i + 1, (i + 1) % 2):
            cp.start()

    half = i % 2
    for cp in copies(i, half):
        cp.wait()
    o_ref[...] = buf[half].astype(o_ref.dtype)


def _gather(h, idx_flat, *, tg):
    nt, d = h.shape
    n = idx_flat.shape[0]
    kern = functools.partial(_gather_kernel, tg=tg, nsteps=n // tg)
    return pl.pallas_call(
        kern,
        grid_spec=pltpu.PrefetchScalarGridSpec(
            num_scalar_prefetch=1,
            grid=(n // tg,),
            in_specs=[pl.BlockSpec(memory_space=pl.ANY)],
            out_specs=pl.BlockSpec((tg, d), lambda i, idx: (i, 0)),
            scratch_shapes=[pltpu.VMEM((2, tg, d), F32), pltpu.SemaphoreType.DMA((2,))],
        ),
        out_shape=jax.ShapeDtypeStruct((n, d), BF16),
        compiler_params=_cparams(("arbitrary",)),
        name="token_gather",
    )(idx_flat, h)


def _ffn_kernel(x_ref, gate_ref, wg_ref, wu_ref, wd_ref, o_ref, *, nf):
    f = pl.program_id(2)

    @pl.when(f == 0)
    def _():
        o_ref[0] = jnp.zeros(o_ref.shape[1:], F32)

    x = x_ref[0]
    g = jnp.dot(x, wg_ref[0, 0].astype(BF16), preferred_element_type=F32)
    u = jnp.dot(x, wu_ref[0, 0].astype(BF16), preferred_element_type=F32)
    a = (g * jax.nn.sigmoid(g)) * u
    o_ref[0] += jnp.dot(a.astype(BF16), wd_ref[0, 0].astype(BF16), preferred_element_type=F32)

    @pl.when(f == nf - 1)
    def _():
        o_ref[0] = o_ref[0] * gate_ref[0][:, 0:1]


def _ffn(xg, gate, w_gate, w_up, w_down, layer, *, tm, tf):
    e, capt, d = xg.shape
    dff = w_gate.shape[-1]
    nf = dff // tf
    kern = functools.partial(_ffn_kernel, nf=nf)
    return pl.pallas_call(
        kern,
        grid=(e, capt // tm, nf),
        in_specs=[pl.BlockSpec((1, tm, d), lambda ei, i, f: (ei, i, 0)),
                  pl.BlockSpec((1, tm, ROW_TOKENS), lambda ei, i, f: (ei, i, 0)),
                  pl.BlockSpec((1, 1, d, tf), lambda ei, i, f: (layer, ei, 0, f)),
                  pl.BlockSpec((1, 1, d, tf), lambda ei, i, f: (layer, ei, 0, f)),
                  pl.BlockSpec((1, 1, tf, d), lambda ei, i, f: (layer, ei, f, 0))],
        out_specs=pl.BlockSpec((1, tm, d), lambda ei, i, f: (ei, i, 0)),
        out_shape=jax.ShapeDtypeStruct((e, capt, d), F32),
        compiler_params=_cparams(("parallel", "parallel", "arbitrary")),
        name="expert_ffn",
    )(xg, gate, w_gate, w_up, w_down)


def _combine_kernel(idx_ref, y_ref, x_hbm, o_hbm, buf, sem_in, sem_out, *, tg, nsteps, steps_per_expert):
    del x_hbm
    i = pl.program_id(0)
    half = i % 2
    other = 1 - half

    def reads(step, hf):
        return [_row_copy(o_hbm, buf.at[hf], sem_in.at[hf], idx_ref[step * tg + s], s) for s in range(tg)]

    def writes(step, hf):
        return [pltpu.make_async_copy(buf.at[hf].at[pl.ds(s, 1)], o_hbm.at[pl.ds(idx_ref[step * tg + s], 1)],
                                      sem_out.at[hf]) for s in range(tg)]

    @pl.when(i > 0)
    def _():
        for cp in writes(i - 1, other):
            cp.wait()

    @pl.when(i % steps_per_expert == 0)
    def _():
        for cp in reads(i, half):
            cp.start()

    @pl.when(((i + 1) % steps_per_expert != 0) & (i + 1 < nsteps))
    def _():
        for cp in reads(i + 1, other):
            cp.start()

    for cp in reads(i, half):
        cp.wait()
    buf[half] = buf[half] + y_ref[...]
    for cp in writes(i, half):
        cp.start()

    @pl.when(i == nsteps - 1)
    def _():
        for cp in writes(i, half):
            cp.wait()


def _combine(x, yeg, idx_flat, *, tg, rows_per_expert):
    nt, d = x.shape
    n = idx_flat.shape[0]
    assert rows_per_expert % tg == 0
    kern = functools.partial(_combine_kernel, tg=tg, nsteps=n // tg, steps_per_expert=rows_per_expert // tg)
    return pl.pallas_call(
        kern,
        grid_spec=pltpu.PrefetchScalarGridSpec(
            num_scalar_prefetch=1,
            grid=(n // tg,),
            in_specs=[pl.BlockSpec((tg, d), lambda i, idx: (i, 0)),
                      pl.BlockSpec(memory_space=pl.ANY)],
            out_specs=pl.BlockSpec(memory_space=pl.ANY),
            scratch_shapes=[pltpu.VMEM((2, tg, d), F32), pltpu.SemaphoreType.DMA((2,)),
                            pltpu.SemaphoreType.DMA((2,))],
        ),
        out_shape=jax.ShapeDtypeStruct((nt, d), F32),
        input_output_aliases={2: 0},
        compiler_params=_cparams(("arbitrary",)),
        name="expert_combine",
    )(idx_flat, yeg, x)


def _moe(x, segs, g, w_router_l, w_gate, w_up, w_down, layer):
    nt, d = x.shape
    h, aff_t = _router(x, g, jnp.swapaxes(w_router_l, 0, 1).astype(BF16), tm=512)
    ids, gates = [], []
    row = 0
    for rows, _ in segs:
        cap = 2 * rows // N_EXP
        idx, gate = _select(aff_t[:, row:row + rows], cap=cap)
        ids.append(idx[:, :, 0] + row)
        gates.append(gate)
        row += rows
    idx_all = jnp.concatenate(ids, axis=1)
    gate_all = jnp.concatenate(gates, axis=1)
    capt = idx_all.shape[1]
    idx_flat = idx_all.reshape(-1)
    tg = 256 if capt % 256 == 0 else 128
    xg = _gather(h, idx_flat, tg=tg).reshape(N_EXP, capt, d)
    tm = next((c for c in (1280, 1024) if capt % c == 0), capt)
    yeg = _ffn(xg, gate_all, w_gate, w_up, w_down, layer, tm=tm, tf=256 if w_gate.shape[-1] % 256 == 0 else w_gate.shape[-1])
    return _combine(x, yeg.reshape(N_EXP * capt, d), idx_flat, tg=tg, rows_per_expert=capt)


def _trunk(x, segs, attn_norm, ffn_norm, w_in_ab, w_out_ab, q_norm_a, k_norm_a, q_norm_b, k_norm_b, sink_b,
           w_in_c, w_out_c, q_norm_c, k_norm_c, rpb_c, w_router, w_gate, w_up, w_down):
    depth = attn_norm.shape[0]
    ones = jnp.ones((HEAD,), F32)
    tm_proj = 1024
    for layer in range(depth):
        if layer % 2 == 0:
            e = layer // 2
            gain = jnp.concatenate([jnp.tile(q_norm_a[e], 8), jnp.tile(k_norm_a[e], 2), jnp.tile(ones, 2),
                                    jnp.tile(q_norm_b[e], 8), jnp.tile(k_norm_b[e], 2), jnp.tile(ones, 2)])
            qkv = _proj(x, attn_norm[layer], w_in_ab[e].astype(BF16), gain, segs,
                        tile_kinds=(((0, 2), "rrrr"), ((2, 3), "rrpp"), ((3, 5), "nnnn"), ((5, 6), "nnpp")),
                        tm=tm_proj)
            oa = []
            row = 0
            for rows, t in segs:
                oa.append(_gattn(qkv, row_off=row, nseq=rows // t, t=t, tq=min(GATTN_TQ, t),
                                 tk=min(GATTN_TK, t // 2)))
                row += rows
            oa = jnp.concatenate(oa, axis=0)
            slopes = 2.0 ** (-8.0 * jnp.arange(1, 9, dtype=F32) / 8)
            ob = _wattn(qkv, slopes, sink_b[e].astype(F32), segs)
            w_out = w_out_ab[e].astype(BF16)
            ka = oa.shape[1]
            x = _outproj([oa, ob], [w_out[:ka], w_out[ka:]], x, tm=512)
        else:
            c = layer // 2
            gain = jnp.concatenate([jnp.tile(q_norm_c[c], 16), jnp.tile(k_norm_c[c], 16), jnp.tile(ones, 16)])
            qkv = _proj(x, attn_norm[layer], w_in_c[c].astype(BF16), gain, segs,
                        tile_kinds=(((0, 8), "nnnn"), ((8, 12), "pppp")), tm=tm_proj)
            tab = _nbr_bias_table(rpb_c[c])
            o = _nattn(qkv, tab, segs, nheads=16)
            x = _outproj([o], [w_out_c[c].astype(BF16)], x, tm=512)
        x = _moe(x, segs, ffn_norm[layer], w_router[layer], w_gate, w_up, w_down, layer)
    return x


def kernel(x_prompt, x_sample, attn_norm, ffn_norm, w_in_ab, w_out_ab, q_norm_a, k_norm_a, q_norm_b, k_norm_b,
           sink_b, w_in_c, w_out_c, q_norm_c, k_norm_c, rpb_c, w_router, w_gate, w_up, w_down):
    bp, tp, d = x_prompt.shape
    bs, ts, _ = x_sample.shape
    x = jnp.concatenate([x_sample.reshape(bs * ts, d), x_prompt.reshape(bp * tp, d)], axis=0)
    segs = ((bs * ts, ts), (bp * tp, tp))
    y = _trunk(x, segs, attn_norm, ffn_norm, w_in_ab, w_out_ab, q_norm_a, k_norm_a, q_norm_b, k_norm_b,
               sink_b, w_in_c, w_out_c, q_norm_c, k_norm_c, rpb_c, w_router, w_gate, w_up, w_down)
    y_sample = y[:bs * ts].reshape(bs, ts, d)
    y_prompt = y[bs * ts:].reshape(bp, tp, d)
    return (y_prompt, y_sample)
```

```python
import functools

import jax
import jax.numpy as jnp
import numpy as np
from jax import lax
from jax.experimental import pallas as pl
from jax.experimental.pallas import tpu as pltpu

F32 = jnp.float32
BF16 = jnp.bfloat16
I32 = jnp.int32

HEAD = 128
GRID_COLS = 64
QBLK = 128
N_EXP = 16
NORM_EPS = 1e-6
NEG_INF = -1e30
ROPE_THETA = 10000.0
NA_ROWS, NA_COLS = 8, 16
NBR_QROWS = 4
NBR_HEADS_PER_STEP = 4
LOG2E = float(np.log2(np.e))
GATTN_TQ, GATTN_TK = 128, 2048
ROW_TOKENS = 128
VMEM_LIMIT = 56 * 1024 * 1024


def _cparams(sem):
    return pltpu.CompilerParams(dimension_semantics=sem, vmem_limit_bytes=VMEM_LIMIT)


def _rms(x, g):
    r = lax.rsqrt(jnp.mean(x * x, axis=-1, keepdims=True) + NORM_EPS)
    return (x * r) * g


def _swap_halves(y):
    lane = lax.broadcasted_iota(I32, y.shape, 1)
    return jnp.where((lane % 64) < 32, pltpu.roll(y, 96, 1), pltpu.roll(y, 32, 1))


def _proj_kernel(*refs, tile_kinds, has_rope):
    if has_rope:
        x_ref, g_ref, w_ref, gain_ref, cos_ref, sin_ref, o_ref, xn_ref = refs
    else:
        x_ref, g_ref, w_ref, gain_ref, o_ref, xn_ref = refs
    j = pl.program_id(1)

    @pl.when(j == 0)
    def _():
        xn_ref[...] = _rms(x_ref[...], g_ref[...]).astype(BF16)

    def tile(kinds):
        acc = jnp.dot(xn_ref[...], w_ref[...], preferred_element_type=F32)
        for h, kind in enumerate(kinds):
            cols = slice(h * HEAD, (h + 1) * HEAD)
            y = acc[:, cols]
            if kind in "rn":
                y = _rms(y, gain_ref[:, cols])
            if kind == "r":
                y = y * cos_ref[...] + _swap_halves(y) * sin_ref[...]
            o_ref[:, cols] = y.astype(o_ref.dtype)

    for (lo, hi), kinds in tile_kinds:
        pl.when((j >= lo) & (j < hi))(functools.partial(tile, kinds))


def _rope_tables(t_max):
    quarter = HEAD // 4
    t = jnp.arange(t_max)
    inv = ROPE_THETA ** (-jnp.arange(quarter, dtype=F32) / quarter)
    ang_r = (t // GRID_COLS).astype(F32)[:, None] * inv
    ang_c = (t % GRID_COLS).astype(F32)[:, None] * inv
    cr, sr, cc, sc = jnp.cos(ang_r), jnp.sin(ang_r), jnp.cos(ang_c), jnp.sin(ang_c)
    return (jnp.concatenate([cr, cr, cc, cc], axis=-1),
            jnp.concatenate([-sr, sr, -sc, sc], axis=-1))


def _proj(x, g, w, gain, segs, *, tile_kinds, tm):
    nt, d = x.shape
    nout = w.shape[1]
    tn = HEAD * len(tile_kinds[0][1])
    assert tile_kinds[-1][0][1] * tn == nout
    rope_tiles = any("r" in kinds for _, kinds in tile_kinds)
    (rows0, t0), (rows1, t1) = segs
    nb0 = rows0 // tm

    def pos_block(i):
        return jnp.where(i < nb0, i % (t0 // tm), (i - nb0) % (t1 // tm))

    in_specs = [
        pl.BlockSpec((tm, d), lambda i, j: (i, 0)),
        pl.BlockSpec((1, d), lambda i, j: (0, 0)),
        pl.BlockSpec((d, tn), lambda i, j: (0, j)),
        pl.BlockSpec((1, tn), lambda i, j: (0, j)),
    ]
    args = [x, g.reshape(1, d), w, gain.reshape(1, nout)]
    if rope_tiles:
        cos, sin = _rope_tables(max(t0, t1))
        in_specs += [pl.BlockSpec((tm, HEAD), lambda i, j: (pos_block(i), 0))] * 2
        args += [cos, sin]
    kern = functools.partial(_proj_kernel, tile_kinds=tile_kinds, has_rope=rope_tiles)
    return pl.pallas_call(
        kern,
        grid=(nt // tm, nout // tn),
        in_specs=in_specs,
        out_specs=pl.BlockSpec((tm, tn), lambda i, j: (i, j)),
        out_shape=jax.ShapeDtypeStruct((nt, nout), BF16),
        scratch_shapes=[pltpu.VMEM((tm, d), BF16)],
        compiler_params=_cparams(("parallel", "arbitrary")),
        name="in_proj",
    )(*args)


def _stack_heads(q, n):
    return jnp.concatenate([q[:, g * HEAD:(g + 1) * HEAD] for g in range(n)], axis=0)


def _gattn_kernel(q_ref, k_ref, v_ref, o_ref, s0_ref, s1_ref, m_ref, l_ref, acc_ref, *, tq, tk, nkc, group):
    c2 = (HEAD ** -0.5) * float(np.log2(np.e))
    q4 = _stack_heads(q_ref[...], group)
    m_ref[...] = jnp.full(m_ref.shape, -jnp.inf, F32)
    l_ref[...] = jnp.zeros(l_ref.shape, F32)
    acc_ref[...] = jnp.zeros(acc_ref.shape, F32)
    nlane = tk // HEAD

    def qk(c, dst_ref):
        kc = k_ref[pl.ds(pl.multiple_of(c * tk, tk), tk), :]
        dst_ref[...] = lax.dot_general(q4, kc, (((1,), (1,)), ((), ())), preferred_element_type=F32)

    def softmax_pv(c, src_ref):
        s = src_ref[...]
        lane_max = functools.reduce(jnp.maximum, [s[:, j * HEAD:(j + 1) * HEAD] for j in range(nlane)])
        m_prev = m_ref[...]
        m_new = jnp.maximum(m_prev, jnp.max(lane_max, axis=-1, keepdims=True) * c2)
        alpha = jnp.exp2(m_prev - m_new)
        p = jnp.exp2(s * c2 - m_new)
        lane_sum = functools.reduce(jnp.add, [p[:, j * HEAD:(j + 1) * HEAD] for j in range(nlane)])
        l_ref[...] = alpha * l_ref[...] + lane_sum
        vc = v_ref[pl.ds(pl.multiple_of(c * tk, tk), tk), :]
        acc_ref[...] = alpha * acc_ref[...] + jnp.dot(p.astype(BF16), vc, preferred_element_type=F32)
        m_ref[...] = m_new

    qk(0, s0_ref)

    def pair(c, carry):
        qk(2 * c + 1, s1_ref)
        softmax_pv(2 * c, s0_ref)
        qk(2 * c + 2, s0_ref)
        softmax_pv(2 * c + 1, s1_ref)
        return carry

    lax.fori_loop(0, nkc // 2 - 1, pair, 0)
    qk(nkc - 1, s1_ref)
    softmax_pv(nkc - 2, s0_ref)
    softmax_pv(nkc - 1, s1_ref)
    out = acc_ref[...] * (1.0 / jnp.sum(l_ref[...], axis=-1, keepdims=True))
    for g in range(group):
        o_ref[:, g * HEAD:(g + 1) * HEAD] = out[g * tq:(g + 1) * tq].astype(o_ref.dtype)


def _gattn(qkv, *, row_off, nseq, t, tq, tk):
    group, kvh = 4, 2
    nq = t // tq
    nkc = t // tk
    assert nkc % 2 == 0
    qoff = row_off // tq
    soff = row_off // t
    qw = group * HEAD
    kern = functools.partial(_gattn_kernel, tq=tq, tk=tk, nkc=nkc, group=group)
    return pl.pallas_call(
        kern,
        grid=(nseq, kvh, nq),
        in_specs=[pl.BlockSpec((tq, qw), lambda b, h, i: (qoff + b * nq + i, h)),
                  pl.BlockSpec((t, HEAD), lambda b, h, i: (soff + b, 8 + h)),
                  pl.BlockSpec((t, HEAD), lambda b, h, i: (soff + b, 10 + h))],
        out_specs=pl.BlockSpec((tq, qw), lambda b, h, i: (b * nq + i, h)),
        out_shape=jax.ShapeDtypeStruct((nseq * t, kvh * qw), BF16),
        scratch_shapes=[pltpu.VMEM((group * tq, tk), F32), pltpu.VMEM((group * tq, tk), F32),
                        pltpu.VMEM((group * tq, 1), F32), pltpu.VMEM((group * tq, HEAD), F32),
                        pltpu.VMEM((group * tq, HEAD), F32)],
        compiler_params=_cparams(("parallel", "parallel", "arbitrary")),
        name="global_attn",
    )(qkv, qkv, qkv)


def _seq_block(b, blocks):
    (n0, s0), (_, s1) = blocks
    first = b < n0
    i = jnp.where(first, b % s0, (b - n0) % s1)
    return i, jnp.where(first, s0, s1)


def _neighbour_block(b, d, blocks):
    i, n = _seq_block(b, blocks)
    return b - i + jnp.clip(i + d, 0, n - 1)


def _wattn_kernel(slope_ref, sink_ref, q_ref, kp_ref, kc_ref, kn_ref, vp_ref, vc_ref, vn_ref, o_ref,
                  *, blocks, scale, group):
    kh = pl.program_id(0)
    i, nb = _seq_block(pl.program_id(1), blocks)
    q4 = _stack_heads(q_ref[...], group)
    kcat = jnp.concatenate([kp_ref[...], kc_ref[...], kn_ref[...]], axis=0)
    vcat = jnp.concatenate([vp_ref[...], vc_ref[...], vn_ref[...]], axis=0)
    rows, span = group * QBLK, 3 * QBLK
    s = lax.dot_general(q4, kcat, (((1,), (1,)), ((), ())), preferred_element_type=F32) * scale
    a = lax.broadcasted_iota(I32, (rows, span), 0) % QBLK
    jj = lax.broadcasted_iota(I32, (rows, span), 1)
    rel = a + QBLK - jj
    dist = jnp.abs(rel)
    ok = (dist <= QBLK) & ((jj >= QBLK) | (i > 0)) & ((jj < 2 * QBLK) | (i < nb - 1))
    rgrp = lax.broadcasted_iota(I32, (rows, 1), 0) // QBLK
    slope = jnp.zeros((rows, 1), F32)
    sink = jnp.zeros((rows, 1), F32)
    for g in range(group):
        slope = jnp.where(rgrp == g, slope_ref[kh * group + g], slope)
        sink = jnp.where(rgrp == g, sink_ref[kh * group + g], sink)
    s = s + (-slope) * dist.astype(F32)
    s = jnp.where(ok, s, NEG_INF)
    m = jnp.maximum(jnp.max(s, axis=-1, keepdims=True), sink)
    p = jnp.exp(s - m)
    denom = jnp.sum(p, axis=-1, keepdims=True) + jnp.exp(sink - m)
    p = p * (1.0 / denom)
    out = jnp.dot(p.astype(BF16), vcat, preferred_element_type=F32)
    for g in range(group):
        o_ref[:, g * HEAD:(g + 1) * HEAD] = out[g * QBLK:(g + 1) * QBLK].astype(o_ref.dtype)


def _wattn(qkv, slopes, sink, segs):
    nt = qkv.shape[0]
    group, kvh = 4, 2
    qw = group * HEAD
    blocks = tuple((rows // QBLK, t // QBLK) for rows, t in segs)

    def kv_spec(col0, d):
        return pl.BlockSpec((QBLK, HEAD), lambda h, b: (_neighbour_block(b, d, blocks), col0 + h))

    smem = pl.BlockSpec(memory_space=pltpu.SMEM)
    kern = functools.partial(_wattn_kernel, blocks=blocks, scale=HEAD ** -0.5, group=group)
    return pl.pallas_call(
        kern,
        grid=(kvh, nt // QBLK),
        in_specs=[smem, smem,
                  pl.BlockSpec((QBLK, qw), lambda h, b: (b, 3 + h)),
                  kv_spec(20, -1), kv_spec(20, 0), kv_spec(20, 1),
                  kv_spec(22, -1), kv_spec(22, 0), kv_spec(22, 1)],
        out_specs=pl.BlockSpec((QBLK, qw), lambda h, b: (b, h)),
        out_shape=jax.ShapeDtypeStruct((nt, kvh * qw), BF16),
        compiler_params=_cparams(("parallel", "arbitrary")),
        name="window_attn",
    )(slopes, sink, qkv, qkv, qkv, qkv, qkv, qkv, qkv)


def _nattn_kernel(q_ref, kp_ref, kc_ref, kn_ref, vp_ref, vc_ref, vn_ref, b_ref, o_ref, *, c2, nh):
    for h in range(nh):
        cols = slice(h * HEAD, (h + 1) * HEAD)
        kcat = jnp.concatenate([kp_ref[:, cols], kc_ref[:, cols], kn_ref[:, cols]], axis=0)
        vcat = jnp.concatenate([vp_ref[:, cols], vc_ref[:, cols], vn_ref[:, cols]], axis=0)
        s = lax.dot_general(q_ref[:, cols], kcat, (((1,), (1,)), ((), ())), preferred_element_type=F32)
        s = s * c2 + b_ref[h]
        nlane = s.shape[1] // HEAD
        lane_max = functools.reduce(jnp.maximum, [s[:, j * HEAD:(j + 1) * HEAD] for j in range(nlane)])
        p = jnp.exp2(s - jnp.max(lane_max, axis=-1, keepdims=True))
        lane_sum = functools.reduce(jnp.add, [p[:, j * HEAD:(j + 1) * HEAD] for j in range(nlane)])
        l = jnp.sum(lane_sum, axis=-1, keepdims=True)
        out = jnp.dot(p.astype(BF16), vcat, preferred_element_type=F32) * (1.0 / l)
        o_ref[:, cols] = out.astype(o_ref.dtype)


def _nbr_bias_table(rpb):
    qr_n, rows = NBR_QROWS, 3 * NBR_QROWS
    blk = np.arange(3)[:, None, None]
    r = blk * qr_n + np.arange(qr_n)[None, :, None]
    kr = (blk - 1) * qr_n + np.arange(3 * qr_n)[None, None, :]
    rs = np.clip(r - NA_ROWS // 2, 0, rows - NA_ROWS)
    row_ok = (kr >= rs) & (kr < rs + NA_ROWS)
    dr = np.clip(kr - r + (NA_ROWS - 1), 0, 2 * NA_ROWS - 2)
    qc = np.arange(GRID_COLS)[:, None]
    kc = np.arange(GRID_COLS)[None, :]
    cs = np.clip(qc - NA_COLS // 2, 0, GRID_COLS - NA_COLS)
    col_ok = (kc >= cs) & (kc < cs + NA_COLS)
    dc = np.clip(kc - qc, -(NA_COLS - 1), NA_COLS - 1) + (NA_COLS - 1)
    h = rpb.shape[0]
    by_col = jnp.take(rpb.astype(F32), jnp.asarray(dc.reshape(-1)), axis=2)
    by_row = jnp.take(by_col, jnp.asarray(dr.reshape(-1)), axis=1)
    bias = by_row.reshape(h, 3, qr_n, 3 * qr_n, GRID_COLS, GRID_COLS)
    ok = row_ok[None, :, :, :, None, None] & col_ok[None, None, None, None, :, :]
    bias = jnp.where(ok, bias * LOG2E, NEG_INF)
    bias = jnp.transpose(bias, (1, 0, 2, 4, 3, 5))
    return bias.reshape(3, h, qr_n * GRID_COLS, 3 * qr_n * GRID_COLS)


def _nattn(qkv, bias_tab, segs, *, nheads):
    nt = qkv.shape[0]
    tq = NBR_QROWS * GRID_COLS
    nh = NBR_HEADS_PER_STEP
    hw = nh * HEAD
    ng = nheads // nh
    blocks = tuple((rows // tq, t // tq) for rows, t in segs)
    assert all(per_seq >= 3 for _, per_seq in blocks)

    def kv_spec(g0, d):
        return pl.BlockSpec((tq, hw), lambda g, b: (_neighbour_block(b, d, blocks), g0 + g))

    def bias_idx(g, b):
        j, nj = _seq_block(b, blocks)
        return (jnp.where(j == 0, 0, jnp.where(j == nj - 1, 2, 1)), g, 0, 0)

    kern = functools.partial(_nattn_kernel, c2=(HEAD ** -0.5) * LOG2E, nh=nh)
    return pl.pallas_call(
        kern,
        grid=(ng, nt // tq),
        in_specs=[pl.BlockSpec((tq, hw), lambda g, b: (b, g)),
                  kv_spec(ng, -1), kv_spec(ng, 0), kv_spec(ng, 1),
                  kv_spec(2 * ng, -1), kv_spec(2 * ng, 0), kv_spec(2 * ng, 1),
                  pl.BlockSpec((None, nh, tq, 3 * tq), bias_idx)],
        out_specs=pl.BlockSpec((tq, hw), lambda g, b: (b, g)),
        out_shape=jax.ShapeDtypeStruct((nt, nheads * HEAD), BF16),
        compiler_params=_cparams(("parallel", "arbitrary")),
        name="nbr_attn",
    )(*([qkv] * 7 + [bias_tab]))


def _outproj_kernel(*refs):
    *aw, x_ref, o_ref = refs
    n = len(aw) // 2
    y = x_ref[...]
    for a_ref, w_ref in zip(aw[:n], aw[n:]):
        y = y + jnp.dot(a_ref[...], w_ref[...], preferred_element_type=F32)
    o_ref[...] = y


def _outproj(parts, weights, x, *, tm):
    nt, d = x.shape
    in_specs = ([pl.BlockSpec((tm, a.shape[1]), lambda i: (i, 0)) for a in parts]
                + [pl.BlockSpec(w.shape, lambda i: (0, 0)) for w in weights]
                + [pl.BlockSpec((tm, d), lambda i: (i, 0))])
    return pl.pallas_call(
        _outproj_kernel,
        grid=(nt // tm,),
        in_specs=in_specs,
        out_specs=pl.BlockSpec((tm, d), lambda i: (i, 0)),
        out_shape=jax.ShapeDtypeStruct((nt, d), F32),
        compiler_params=_cparams(("parallel",)),
        name="out_proj",
    )(*parts, *weights, x)


def _router_kernel(x_ref, g_ref, wr_ref, h_ref, aff_ref):
    h = _rms(x_ref[...], g_ref[...])
    h_ref[...] = h
    logits = lax.dot_general(wr_ref[...], h.astype(BF16), (((1,), (1,)), ((), ())),
                             preferred_element_type=F32)
    m = jnp.max(logits, axis=0, keepdims=True)
    p = jnp.exp(logits - m)
    aff_ref[...] = p / jnp.sum(p, axis=0, keepdims=True)


def _router(x, g, wr_t, *, tm):
    nt, d = x.shape
    e = wr_t.shape[0]
    return pl.pallas_call(
        _router_kernel,
        grid=(nt // tm,),
        in_specs=[pl.BlockSpec((tm, d), lambda i: (i, 0)),
                  pl.BlockSpec((1, d), lambda i: (0, 0)),
                  pl.BlockSpec((e, d), lambda i: (0, 0))],
        out_specs=[pl.BlockSpec((tm, d), lambda i: (i, 0)),
                   pl.BlockSpec((e, tm), lambda i: (0, i))],
        out_shape=[jax.ShapeDtypeStruct((nt, d), F32), jax.ShapeDtypeStruct((e, nt), F32)],
        compiler_params=_cparams(("parallel",)),
        name="router",
    )(x, g.reshape(1, d), wr_t)


def _split3(a):
    hi = a.astype(BF16)
    r1 = a - hi.astype(F32)
    mid = r1.astype(BF16)
    lo = (r1 - mid.astype(F32)).astype(BF16)
    return hi, mid, lo


def _select_kernel(a_ref, idx_ref, gate_ref, *, cap, nrows):
    a = a_ref[0]
    shape = a.shape
    ones_col = jnp.ones((nrows, ROW_TOKENS), BF16)

    def ind(mask):
        return jnp.where(mask, 1.0, 0.0)

    def count(mask):
        return jnp.sum(jnp.sum(ind(mask), axis=0, keepdims=True), axis=1, keepdims=True)

    def step(_, st):
        lo, hi, found, tf = st
        mid = lo + ((hi - lo) >> 1)
        thr = lax.bitcast_convert_type(mid, F32)
        c = count(a >= thr)
        ge = c >= cap
        hit = (c == cap) & (found == 0)
        return (jnp.where(ge, mid, lo), jnp.where(ge, hi, mid),
                jnp.where(hit, 1, found), jnp.where(hit, mid, tf))

    z = jnp.zeros((1, 1), I32)
    lo, hi, found, tf = lax.fori_loop(0, 31, step, (z, z + 0x7F800000, z, z))
    v = lax.bitcast_convert_type(lo, F32)
    gt = a > v
    eq = a == v
    need = cap - count(gt)
    sub = lax.broadcasted_iota(I32, (ROW_TOKENS, ROW_TOKENS), 0)
    lane = lax.broadcasted_iota(I32, (ROW_TOKENS, ROW_TOKENS), 1)
    l_incl = ind(lane <= sub).astype(BF16)
    rr = lax.broadcasted_iota(I32, (nrows, nrows), 0)
    rc = lax.broadcasted_iota(I32, (nrows, nrows), 1)
    u_strict = ind(rr < rc).astype(BF16)

    def prefix(mask_bf):
        cin = jnp.dot(l_incl, mask_bf, preferred_element_type=F32)
        tot = jnp.broadcast_to(cin[ROW_TOKENS - 1:ROW_TOKENS, :], (8, nrows))
        cex = jnp.dot(tot.astype(BF16), u_strict, preferred_element_type=F32)[0:1, :]
        return cin, cin[ROW_TOKENS - 1:ROW_TOKENS, :], cex

    eq_f = ind(eq)
    ceq, _, ceq_off = prefix(eq_f.astype(BF16))
    eq_rank = ceq - eq_f + ceq_off
    sel_tie = ind(gt | (eq & (eq_rank < need)))
    sel = jnp.where(found > 0, ind(a >= lax.bitcast_convert_type(tf, F32)), sel_tie)

    cin, tot, cex = prefix(sel.astype(BF16))
    cinc = cex + tot
    pos = lax.broadcasted_iota(I32, (cap, nrows), 0).astype(F32)
    before = cinc <= pos
    before_bf = jnp.where(before, 1.0, 0.0).astype(BF16)
    rstar = jnp.dot(before_bf, ones_col, preferred_element_type=F32)
    skipped = jnp.dot(jnp.where(before, tot, 0.0).astype(BF16), ones_col, preferred_element_type=F32)
    pe = lax.broadcasted_iota(I32, (cap, ROW_TOKENS), 0).astype(F32) - skipped
    rid = lax.broadcasted_iota(I32, (cap, nrows), 1).astype(F32)
    rowsel = jnp.where(rid == rstar[:, 0:1], 1.0, 0.0).astype(BF16)
    nt_dims = (((1,), (1,)), ((), ()))
    cin_g = lax.dot_general(rowsel, cin.astype(BF16), nt_dims, preferred_element_type=F32)
    lstar = jnp.dot(jnp.where(cin_g <= pe, 1.0, 0.0).astype(BF16),
                    jnp.ones((ROW_TOKENS, ROW_TOKENS), BF16), preferred_element_type=F32)
    idx_ref[0] = (rstar * ROW_TOKENS + lstar).astype(I32)
    a_g = sum(lax.dot_general(rowsel, part, nt_dims, preferred_element_type=F32) for part in _split3(a))
    lid = lax.broadcasted_iota(I32, (cap, ROW_TOKENS), 1).astype(F32)
    gate = jnp.sum(jnp.where(lid == lstar, a_g, 0.0), axis=1, keepdims=True)
    gate_ref[0] = jnp.broadcast_to(gate, (cap, ROW_TOKENS))


def _select(aff_t, *, cap):
    e, n = aff_t.shape
    r = n // ROW_TOKENS
    nrows = max(r, ROW_TOKENS)
    a2 = jnp.swapaxes(aff_t.reshape(e, r, ROW_TOKENS), 1, 2)
    if nrows > r:
        a2 = jnp.pad(a2, ((0, 0), (0, 0), (0, nrows - r)), constant_values=-1.0)
    kern = functools.partial(_select_kernel, cap=cap, nrows=nrows)
    return pl.pallas_call(
        kern,
        grid=(e,),
        in_specs=[pl.BlockSpec((1, ROW_TOKENS, nrows), lambda i: (i, 0, 0))],
        out_specs=[pl.BlockSpec((1, cap, ROW_TOKENS), lambda i: (i, 0, 0)),
                   pl.BlockSpec((1, cap, ROW_TOKENS), lambda i: (i, 0, 0))],
        out_shape=[jax.ShapeDtypeStruct((e, cap, ROW_TOKENS), I32),
                   jax.ShapeDtypeStruct((e, cap, ROW_TOKENS), F32)],
        compiler_params=_cparams(("parallel",)),
        name="expert_select",
    )(a2)


def _row_copy(src_hbm, dst_vmem, sem, token, slot):
    return pltpu.make_async_copy(src_hbm.at[pl.ds(token, 1)], dst_vmem.at[pl.ds(slot, 1)], sem)


def _gather_kernel(idx_ref, h_hbm, o_ref, buf, sem, *, tg, nsteps):
    i = pl.program_id(0)

    def copies(step, half):
        return [_row_copy(h_hbm, buf.at[half], sem.at[half], idx_ref[step * tg + s], s) for s in range(tg)]

    @pl.when(i == 0)
    def _():
        for cp in copies(0, 0):
            cp.start()

    @pl.when(i + 1 < nsteps)
    def _():
        for cp in copies(i + 1, (i + 1) % 2):
            cp.start()

    half = i % 2
    for cp in copies(i, half):
        cp.wait()
    o_ref[...] = buf[half].astype(o_ref.dtype)


def _gather(h, idx_flat, *, tg):
    nt, d = h.shape
    n = idx_flat.shape[0]
    kern = functools.partial(_gather_kernel, tg=tg, nsteps=n // tg)
    return pl.pallas_call(
        kern,
        grid_spec=pltpu.PrefetchScalarGridSpec(
            num_scalar_prefetch=1,
            grid=(n // tg,),
            in_specs=[pl.BlockSpec(memory_space=pl.ANY)],
            out_specs=pl.BlockSpec((tg, d), lambda i, idx: (i, 0)),
            scratch_shapes=[pltpu.VMEM((2, tg, d), F32), pltpu.SemaphoreType.DMA((2,))],
        ),
        out_shape=jax.ShapeDtypeStruct((n, d), BF16),
        compiler_params=_cparams(("arbitrary",)),
        name="token_gather",
    )(idx_flat, h)


def _ffn_kernel(x_ref, gate_ref, wg_ref, wu_ref, wd_ref, o_ref, *, nf):
    f = pl.program_id(2)

    @pl.when(f == 0)
    def _():
        o_ref[0] = jnp.zeros(o_ref.shape[1:], F32)

    x = x_ref[0]
    g = jnp.dot(x, wg_ref[0, 0].astype(BF16), preferred_element_type=F32)
    u = jnp.dot(x, wu_ref[0, 0].astype(BF16), preferred_element_type=F32)
    a = (g * jax.nn.sigmoid(g)) * u
    o_ref[0] += jnp.dot(a.astype(BF16), wd_ref[0, 0].astype(BF16), preferred_element_type=F32)

    @pl.when(f == nf - 1)
    def _():
        o_ref[0] = o_ref[0] * gate_ref[0][:, 0:1]


def _ffn(xg, gate, w_gate, w_up, w_down, layer, *, tm, tf):
    e, capt, d = xg.shape
    dff = w_gate.shape[-1]
    nf = dff // tf
    kern = functools.partial(_ffn_kernel, nf=nf)
    return pl.pallas_call(
        kern,
        grid=(e, capt // tm, nf),
        in_specs=[pl.BlockSpec((1, tm, d), lambda ei, i, f: (ei, i, 0)),
                  pl.BlockSpec((1, tm, ROW_TOKENS), lambda ei, i, f: (ei, i, 0)),
                  pl.BlockSpec((1, 1, d, tf), lambda ei, i, f: (layer, ei, 0, f)),
                  pl.BlockSpec((1, 1, d, tf), lambda ei, i, f: (layer, ei, 0, f)),
                  pl.BlockSpec((1, 1, tf, d), lambda ei, i, f: (layer, ei, f, 0))],
        out_specs=pl.BlockSpec((1, tm, d), lambda ei, i, f: (ei, i, 0)),
        out_shape=jax.ShapeDtypeStruct((e, capt, d), F32),
        compiler_params=_cparams(("parallel", "parallel", "arbitrary")),
        name="expert_ffn",
    )(xg, gate, w_gate, w_up, w_down)


def _combine_kernel(idx_ref, y_ref, x_hbm, o_hbm, buf, sem_in, sem_out, *, tg, nsteps, steps_per_expert):
    del x_hbm
    i = pl.program_id(0)
    half = i % 2
    other = 1 - half

    def reads(step, hf):
        return [_row_copy(o_hbm, buf.at[hf], sem_in.at[hf], idx_ref[step * tg + s], s) for s in range(tg)]

    def writes(step, hf):
        return [pltpu.make_async_copy(buf.at[hf].at[pl.ds(s, 1)], o_hbm.at[pl.ds(idx_ref[step * tg + s], 1)],
                                      sem_out.at[hf]) for s in range(tg)]

    @pl.when(i > 0)
    def _():
        for cp in writes(i - 1, other):
            cp.wait()

    @pl.when(i % steps_per_expert == 0)
    def _():
        for cp in reads(i, half):
            cp.start()

    @pl.when(((i + 1) % steps_per_expert != 0) & (i + 1 < nsteps))
    def _():
        for cp in reads(i + 1, other):
            cp.start()

    for cp in reads(i, half):
        cp.wait()
    buf[half] = buf[half] + y_ref[...]
    for cp in writes(i, half):
        cp.start()

    @pl.when(i == nsteps - 1)
    def _():
        for cp in writes(i, half):
            cp.wait()


def _combine(x, yeg, idx_flat, *, tg, rows_per_expert):
    nt, d = x.shape
    n = idx_flat.shape[0]
    assert rows_per_expert % tg == 0
    kern = functools.partial(_combine_kernel, tg=tg, nsteps=n // tg, steps_per_expert=rows_per_expert // tg)
    return pl.pallas_call(
        kern,
        grid_spec=pltpu.PrefetchScalarGridSpec(
            num_scalar_prefetch=1,
            grid=(n // tg,),
            in_specs=[pl.BlockSpec((tg, d), lambda i, idx: (i, 0)),
                      pl.BlockSpec(memory_space=pl.ANY)],
            out_specs=pl.BlockSpec(memory_space=pl.ANY),
            scratch_shapes=[pltpu.VMEM((2, tg, d), F32), pltpu.SemaphoreType.DMA((2,)),
                            pltpu.SemaphoreType.DMA((2,))],
        ),
        out_shape=jax.ShapeDtypeStruct((nt, d), F32),
        input_output_aliases={2: 0},
        compiler_params=_cparams(("arbitrary",)),
        name="expert_combine",
    )(idx_flat, yeg, x)


def _moe(x, segs, g, w_router_l, w_gate, w_up, w_down, layer):
    nt, d = x.shape
    h, aff_t = _router(x, g, jnp.swapaxes(w_router_l, 0, 1).astype(BF16), tm=512)
    ids, gates = [], []
    row = 0
    for rows, _ in segs:
        cap = 2 * rows // N_EXP
        idx, gate = _select(aff_t[:, row:row + rows], cap=cap)
        ids.append(idx[:, :, 0] + row)
        gates.append(gate)
        row += rows
    idx_all = jnp.concatenate(ids, axis=1)
    gate_all = jnp.concatenate(gates, axis=1)
    capt = idx_all.shape[1]
    idx_flat = idx_all.reshape(-1)
    tg = 256 if capt % 256 == 0 else 128
    xg = _gather(h, idx_flat, tg=tg).reshape(N_EXP, capt, d)
    tm = next((c for c in (1280, 1024) if capt % c == 0), capt)
    yeg = _ffn(xg, gate_all, w_gate, w_up, w_down, layer, tm=tm, tf=256 if w_gate.shape[-1] % 256 == 0 else w_gate.shape[-1])
    return _combine(x, yeg.reshape(N_EXP * capt, d), idx_flat, tg=tg, rows_per_expert=capt)


def _trunk(x, segs, attn_norm, ffn_norm, w_in_ab, w_out_ab, q_norm_a, k_norm_a, q_norm_b, k_norm_b, sink_b,
           w_in_c, w_out_c, q_norm_c, k_norm_c, rpb_c, w_router, w_gate, w_up, w_down):
    depth = attn_norm.shape[0]
    ones = jnp.ones((HEAD,), F32)
    tm_proj = 1024
    for layer in range(depth):
        if layer % 2 == 0:
            e = layer // 2
            gain = jnp.concatenate([jnp.tile(q_norm_a[e], 8), jnp.tile(k_norm_a[e], 2), jnp.tile(ones, 2),
                                    jnp.tile(q_norm_b[e], 8), jnp.tile(k_norm_b[e], 2), jnp.tile(ones, 2)])
            qkv = _proj(x, attn_norm[layer], w_in_ab[e].astype(BF16), gain, segs,
                        tile_kinds=(((0, 1), "rrrrrrrr"), ((1, 2), "rrppnnnn"), ((2, 3), "nnnnnnpp")),
                        tm=tm_proj)
            oa = []
            row = 0
            for rows, t in segs:
                oa.append(_gattn(qkv, row_off=row, nseq=rows // t, t=t, tq=min(GATTN_TQ, t),
                                 tk=min(GATTN_TK, t // 2)))
                row += rows
            oa = jnp.concatenate(oa, axis=0)
            slopes = 2.0 ** (-8.0 * jnp.arange(1, 9, dtype=F32) / 8)
            ob = _wattn(qkv, slopes, sink_b[e].astype(F32), segs)
            w_out = w_out_ab[e].astype(BF16)
            ka = oa.shape[1]
            x = _outproj([oa, ob], [w_out[:ka], w_out[ka:]], x, tm=512)
        else:
            c = layer // 2
            gain = jnp.concatenate([jnp.tile(q_norm_c[c], 16), jnp.tile(k_norm_c[c], 16), jnp.tile(ones, 16)])
            qkv = _proj(x, attn_norm[layer], w_in_c[c].astype(BF16), gain, segs,
                        tile_kinds=(((0, 4), "nnnnnnnn"), ((4, 6), "pppppppp")), tm=tm_proj)
            tab = _nbr_bias_table(rpb_c[c])
            o = _nattn(qkv, tab, segs, nheads=16)
            x = _outproj([o], [w_out_c[c].astype(BF16)], x, tm=512)
        x = _moe(x, segs, ffn_norm[layer], w_router[layer], w_gate, w_up, w_down, layer)
    return x


def kernel(x_prompt, x_sample, attn_norm, ffn_norm, w_in_ab, w_out_ab, q_norm_a, k_norm_a, q_norm_b, k_norm_b,
           sink_b, w_in_c, w_out_c, q_norm_c, k_norm_c, rpb_c, w_router, w_gate, w_up, w_down):
    bp, tp, d = x_prompt.shape
    bs, ts, _ = x_sample.shape
    x = jnp.concatenate([x_sample.reshape(bs * ts, d), x_prompt.reshape(bp * tp, d)], axis=0)
    segs = ((bs * ts, ts), (bp * tp, tp))
    y = _trunk(x, segs, attn_norm, ffn_norm, w_in_ab, w_out_ab, q_norm_a, k_norm_a, q_norm_b, k_norm_b,
               sink_b, w_in_c, w_out_c, q_norm_c, k_norm_c, rpb_c, w_router, w_gate, w_up, w_down)
    y_sample = y[:bs * ts].reshape(bs, ts, d)
    y_prompt = y[bs * ts:].reshape(bp, tp, d)
    return (y_prompt, y_sample)
```
